```python
import math
import functools
import jax
import jax.numpy as jnp
from jax import lax
import numpy as np

D_MODEL = 1024
BATCH = 4
SEQ = 4096
DEPTH = 4
DEC_BATCH = 128
DEC_SEQ = 1
PAST_LEN = 2048
PAGE_SIZE = 128

HEAD_DIM = 64
D_CONF = D_MODEL // 4
D_SCONV = D_MODEL // 4
D_ATT = D_MODEL // 2
N_ATT_HEADS = D_ATT // HEAD_DIM
CONF_WIDTH = 31
SCONV_WIDTH = 3
MOBA_BLOCK = 256
MOBA_TOPK = 3
Q_CHUNK = 16
N_BUCKETS = 32
MAX_DISTANCE = 128
N_GROUPS = 4
EXPERTS_PER_GROUP = 4
N_EXPERTS = N_GROUPS * EXPERTS_PER_GROUP
D_EXPERT = D_MODEL // 4
TOP_GROUPS = 1
TOP_EXPERTS = 2
ALPHA = (2 * DEPTH) ** 0.25
BETA = (8 * DEPTH) ** -0.25
LN_EPS = 1e-5
NEG_INF = -1e30
ATT_SCALE = HEAD_DIM ** -0.5
D_IN = 2 * D_CONF + 3 * D_SCONV + 3 * D_ATT
SPLIT_IDX = (D_CONF, 2 * D_CONF, 2 * D_CONF + D_SCONV, 2 * D_CONF + 2 * D_SCONV,
             2 * D_CONF + 3 * D_SCONV, 2 * D_CONF + 3 * D_SCONV + D_ATT,
             2 * D_CONF + 3 * D_SCONV + 2 * D_ATT)

kernel_name = 'hymba_conformer_sconv_moba_hmoe_step'


def layer_norm(x, g, b):
    xf = x.astype(jnp.float32)
    mu = jnp.mean(xf, axis=-1, keepdims=True)
    var = jnp.mean(jnp.square(xf - mu), axis=-1, keepdims=True)
    return ((xf - mu) * lax.rsqrt(var + LN_EPS) * g + b).astype(x.dtype)


def causal_dwconv(u, ctx, w):
    full = jnp.concatenate([ctx, u], axis=1)
    y = lax.conv_general_dilated(full, w[:, None, :], window_strides=(1,), padding='VALID',
                                 dimension_numbers=('NWC', 'WIO', 'NWC'),
                                 feature_group_count=u.shape[-1])
    return y, full[:, full.shape[1] - (w.shape[0] - 1):]


def conformer_conv(a_val, a_gate, ctx, w_dw, b_dw, ln_g, ln_b):
    u = a_val * jax.nn.sigmoid(a_gate)
    y, new_ctx = causal_dwconv(u, ctx, w_dw)
    y = layer_norm(y + b_dw, ln_g, ln_b)
    return jax.nn.silu(y), new_ctx


def short_gated_conv(g_b, g_c, h, ctx, w_sc):
    y, new_ctx = causal_dwconv(g_c * h, ctx, w_sc)
    return g_b * y, new_ctx


def t5_bucket(dist):
    n = jnp.maximum(dist, 0)
    max_exact = N_BUCKETS // 2
    nf = jnp.maximum(n, 1).astype(jnp.float32)
    large = max_exact + (jnp.log(nf / max_exact) / math.log(MAX_DISTANCE / max_exact)
                         * (N_BUCKETS - max_exact)).astype(jnp.int32)
    large = jnp.minimum(large, N_BUCKETS - 1)
    return jnp.where(n < max_exact, n, large)


def rows_to_pages(t):
    bsz, n, h, dh = t.shape
    n_pg = -(-n // PAGE_SIZE)
    t = jnp.pad(t, ((0, 0), (0, n_pg * PAGE_SIZE - n), (0, 0), (0, 0)))
    return t.reshape(bsz, n_pg, PAGE_SIZE, h, dh).transpose(0, 1, 3, 2, 4)


def moba_attend(q, q_pos, block_sum, gather_blocks, rel_bias):
    bsz, tq, nh, dh = q.shape
    nb = block_sum.shape[1]
    ksel = min(MOBA_TOPK, nb)
    own = q_pos // MOBA_BLOCK
    kbar = block_sum / MOBA_BLOCK
    gate = jnp.einsum('bthd,bnhd->bthn', q.astype(jnp.float32), kbar)
    past = jnp.arange(nb)[None, :] < own[:, None]
    gate = jnp.where(past[None, :, None, :], gate, NEG_INF)
    _, sel = lax.top_k(gate, ksel)
    own_b = jnp.broadcast_to(own[None, :, None, None], (bsz, tq, nh, 1)).astype(sel.dtype)
    idx = jnp.concatenate([sel, own_b], axis=-1)
    blk_ok = jnp.concatenate([sel < own[None, :, None, None], jnp.ones(own_b.shape, bool)], axis=-1)
    kg, vg = gather_blocks(idx)
    key_pos = idx[..., None] * MOBA_BLOCK + jnp.arange(MOBA_BLOCK)
    dist = q_pos[None, :, None, None, None] - key_pos
    ok = blk_ok[..., None] & (dist >= 0)
    h_idx = jnp.arange(nh)[None, None, :, None, None]
    bias = rel_bias.T[h_idx, t5_bucket(dist)].astype(jnp.float32)
    logits = jnp.einsum('bthd,bthnkd->bthnk', q, kg, preferred_element_type=jnp.float32) * ATT_SCALE + bias
    logits = jnp.where(ok, logits, NEG_INF)
    p = jax.nn.softmax(logits.reshape(bsz, tq, nh, -1), axis=-1).reshape(logits.shape)
    out = jnp.einsum('bthnk,bthnkd->bthd', p.astype(vg.dtype), vg, preferred_element_type=jnp.float32)
    return out.astype(q.dtype)


def moba_prompt(q, k, v, rel_bias):
    bsz, t, nh, dh = q.shape
    nb = -(-t // MOBA_BLOCK)
    pad = nb * MOBA_BLOCK - t
    kp = jnp.pad(k, ((0, 0), (0, pad), (0, 0), (0, 0))).reshape(bsz, nb, MOBA_BLOCK, nh, dh)
    vp = jnp.pad(v, ((0, 0), (0, pad), (0, 0), (0, 0))).reshape(bsz, nb, MOBA_BLOCK, nh, dh)
    block_sum = jnp.sum(kp, axis=2, dtype=jnp.float32)
    kb = kp.transpose(0, 1, 3, 2, 4)
    vb = vp.transpose(0, 1, 3, 2, 4)
    b_idx = jnp.arange(bsz)[:, None, None, None]
    h_idx = jnp.arange(nh)[None, None, :, None]

    def gather_blocks(idx):
        return kb[b_idx, idx, h_idx], vb[b_idx, idx, h_idx]

    n_chunks = t // Q_CHUNK
    qc = q.reshape(bsz, n_chunks, Q_CHUNK, nh, dh).transpose(1, 0, 2, 3, 4)
    pc = jnp.arange(t, dtype=jnp.int32).reshape(n_chunks, Q_CHUNK)
    out = lax.map(lambda a: moba_attend(a[0], a[1], block_sum, gather_blocks, rel_bias), (qc, pc))
    return out.transpose(1, 0, 2, 3, 4).reshape(bsz, t, nh, dh)


def moba_sample(q, k_new, v_new, cache_k, cache_v, layer, page_table, rel_bias):
    bsz, tn, nh, dh = q.shape
    n_pages = page_table.shape[1]
    ppb = MOBA_BLOCK // PAGE_SIZE
    np_new = -(-tn // PAGE_SIZE)
    n_log = n_pages + np_new
    nb = -(-n_log // ppb)
    pad_pages = nb * ppb - n_log
    kn = rows_to_pages(k_new)
    vn = rows_to_pages(v_new)
    pool_sum = jnp.sum(cache_k[layer], axis=2, dtype=jnp.float32)
    page_sum = jnp.concatenate([pool_sum[page_table], jnp.sum(kn, axis=3, dtype=jnp.float32),
                                jnp.zeros((bsz, pad_pages, nh, dh), jnp.float32)], axis=1)
    block_sum = page_sum.reshape(bsz, nb, ppb, nh, dh).sum(axis=2)
    b_idx = jnp.arange(bsz)[:, None, None, None, None]
    h_idx = jnp.arange(nh)[None, None, :, None, None]

    def gather_blocks(idx):
        pg = idx[..., None] * ppb + jnp.arange(ppb)
        in_past = (pg < n_pages)[..., None, None]
        phys = page_table[b_idx, jnp.clip(pg, 0, n_pages - 1)]
        new_pg = jnp.clip(pg - n_pages, 0, np_new - 1)

        def pick(pool, fresh_pages):
            r = jnp.where(in_past, pool[layer, phys, h_idx], fresh_pages[b_idx, new_pg, h_idx])
            return r.reshape(r.shape[:4] + (MOBA_BLOCK, dh))

        return pick(cache_k, kn), pick(cache_v, vn)

    q_pos = PAST_LEN + jnp.arange(tn, dtype=jnp.int32)
    return moba_attend(q, q_pos, block_sum, gather_blocks, rel_bias)


def hier_moe(x, w_grp, b_grp, w_rt, b_rt, w_gate, w_up, w_down):
    bsz, t = x.shape[0], x.shape[1]
    g_logits = jnp.einsum('btd,dg->btg', x, w_grp, preferred_element_type=jnp.float32) + b_grp
    g_w, g_idx = lax.top_k(jax.nn.softmax(g_logits, axis=-1), TOP_GROUPS)
    e_logits = (jnp.einsum('btd,de->bte', x, w_rt, preferred_element_type=jnp.float32) + b_rt)
    e_logits = e_logits.reshape(bsz, t, N_GROUPS, EXPERTS_PER_GROUP)
    sel_g = jnp.broadcast_to(g_idx[..., None], (bsz, t, TOP_GROUPS, EXPERTS_PER_GROUP))
    e_logits = jnp.take_along_axis(e_logits, sel_g, axis=2)
    e_w, e_idx = lax.top_k(jax.nn.softmax(e_logits, axis=-1), TOP_EXPERTS)
    e_w = e_w / jnp.sum(e_w, axis=-1, keepdims=True)
    weight = g_w[..., None] * e_w
    expert_id = g_idx[..., None] * EXPERTS_PER_GROUP + e_idx
    combine = jnp.sum(jax.nn.one_hot(expert_id, N_EXPERTS, dtype=jnp.float32) * weight[..., None], axis=(2, 3))
    hidden = jax.nn.silu(jnp.einsum('btd,edf->btef', x, w_gate)) * jnp.einsum('btd,edf->btef', x, w_up)
    return jnp.einsum('btef,efd->btd', hidden * combine[..., None].astype(x.dtype), w_down)


def layer_forward(x, ctx_a, ctx_b, attend, p):
    bsz, t = x.shape[0], x.shape[1]
    z = jnp.einsum('btd,de->bte', x, p['w_in'])
    a_val, a_gate, s_b, s_c, s_h, q, k, v = jnp.split(z, list(SPLIT_IDX), axis=-1)
    y_a, new_a = conformer_conv(a_val, a_gate, ctx_a, p['conv_a_w'], p['conv_a_b'], p['conv_a_ln_g'], p['conv_a_ln_b'])
    y_b, new_b = short_gated_conv(s_b, s_c, s_h, ctx_b, p['conv_b_w'])
    q = q.reshape(bsz, t, N_ATT_HEADS, HEAD_DIM)
    k = k.reshape(bsz, t, N_ATT_HEADS, HEAD_DIM)
    v = v.reshape(bsz, t, N_ATT_HEADS, HEAD_DIM)
    y_c = attend(q, k, v).reshape(bsz, t, D_ATT)
    mix = jnp.einsum('bte,ed->btd', jnp.concatenate([y_a, y_b, y_c], axis=-1), p['w_out'])
    x = layer_norm(ALPHA * x + mix, p['ln1_g'], p['ln1_b'])
    ffn = hier_moe(x, p['w_group'], p['b_group'], p['w_router'], p['b_router'], p['w_gate'], p['w_up'], p['w_down'])
    x = layer_norm(ALPHA * x + ffn, p['ln2_g'], p['ln2_b'])
    return x, new_a, new_b, k, v


def setup_inputs(seed: int = 0) -> dict:
    key = jax.random.key(seed)
    ks = jax.random.split(key, 32)
    f32 = jnp.float32
    n_pages = PAST_LEN // PAGE_SIZE
    n_pool = (DEC_BATCH * n_pages * 5) // 4

    def nrm(k, shape, s):
        return jax.random.normal(k, shape, f32) * s

    perm = jax.random.permutation(ks[6], n_pool)[:DEC_BATCH * n_pages]
    return {
        'x_prompt': nrm(ks[0], (BATCH, SEQ, D_MODEL), 1.0),
        'x_sample': nrm(ks[1], (DEC_BATCH, DEC_SEQ, D_MODEL), 1.0),
        'cache_k': nrm(ks[2], (DEPTH, n_pool, N_ATT_HEADS, PAGE_SIZE, HEAD_DIM), 1.0),
        'cache_v': nrm(ks[3], (DEPTH, n_pool, N_ATT_HEADS, PAGE_SIZE, HEAD_DIM), 1.0),
        'state_conv_a': nrm(ks[4], (DEPTH, DEC_BATCH, CONF_WIDTH - 1, D_CONF), 0.5),
        'state_conv_b': nrm(ks[5], (DEPTH, DEC_BATCH, SCONV_WIDTH - 1, D_SCONV), 1.0),
        'page_table': perm.reshape(DEC_BATCH, n_pages).astype(jnp.int32),
        'ln_in_g': 1.0 + nrm(ks[7], (D_MODEL,), 0.02),
        'ln_in_b': nrm(ks[8], (D_MODEL,), 0.02),
        'w_in': nrm(ks[9], (DEPTH, D_MODEL, D_IN), D_MODEL ** -0.5),
        'conv_a_w': nrm(ks[10], (DEPTH, CONF_WIDTH, D_CONF), CONF_WIDTH ** -0.5),
        'conv_a_b': nrm(ks[11], (DEPTH, D_CONF), 0.02),
        'conv_a_ln_g': 1.0 + nrm(ks[12], (DEPTH, D_CONF), 0.02),
        'conv_a_ln_b': nrm(ks[13], (DEPTH, D_CONF), 0.02),
        'conv_b_w': nrm(ks[14], (DEPTH, SCONV_WIDTH, D_SCONV), SCONV_WIDTH ** -0.5),
        'rel_bias': nrm(ks[15], (N_BUCKETS, N_ATT_HEADS), 0.1),
        'w_out': nrm(ks[16], (DEPTH, D_MODEL, D_MODEL), BETA * D_MODEL ** -0.5),
        'ln1_g': 1.0 + nrm(ks[17], (DEPTH, D_MODEL), 0.02),
        'ln1_b': nrm(ks[18], (DEPTH, D_MODEL), 0.02),
        'w_group': nrm(ks[19], (DEPTH, D_MODEL, N_GROUPS), D_MODEL ** -0.5),
        'b_group': nrm(ks[20], (DEPTH, N_GROUPS), 0.01),
        'w_router': nrm(ks[21], (DEPTH, D_MODEL, N_EXPERTS), D_MODEL ** -0.5),
        'b_router': nrm(ks[22], (DEPTH, N_EXPERTS), 0.01),
        'w_gate': nrm(ks[23], (DEPTH, N_EXPERTS, D_MODEL, D_EXPERT), D_MODEL ** -0.5),
        'w_up': nrm(ks[24], (DEPTH, N_EXPERTS, D_MODEL, D_EXPERT), D_MODEL ** -0.5),
        'w_down': nrm(ks[25], (DEPTH, N_EXPERTS, D_EXPERT, D_MODEL), BETA * D_EXPERT ** -0.5),
        'ln2_g': 1.0 + nrm(ks[26], (DEPTH, D_MODEL), 0.02),
        'ln2_b': nrm(ks[27], (DEPTH, D_MODEL), 0.02),
    }


def reference(x_prompt, x_sample, cache_k, cache_v, state_conv_a, state_conv_b, page_table,
              ln_in_g, ln_in_b, w_in, conv_a_w, conv_a_b, conv_a_ln_g, conv_a_ln_b, conv_b_w,
              rel_bias, w_out, ln1_g, ln1_b, w_group, b_group, w_router, b_router,
              w_gate, w_up, w_down, ln2_g, ln2_b):
    h_p = layer_norm(x_prompt, ln_in_g, ln_in_b)
    h_s = layer_norm(x_sample, ln_in_g, ln_in_b)
    zero_a = jnp.zeros((x_prompt.shape[0], CONF_WIDTH - 1, D_CONF), x_prompt.dtype)
    zero_b = jnp.zeros((x_prompt.shape[0], SCONV_WIDTH - 1, D_SCONV), x_prompt.dtype)
    attend_p = functools.partial(moba_prompt, rel_bias=rel_bias)
    kp_l, vp_l, ap_l, bp_l, ks_l, vs_l, as_l, bs_l = [], [], [], [], [], [], [], []
    for l in range(DEPTH):
        p = {'w_in': w_in[l], 'conv_a_w': conv_a_w[l], 'conv_a_b': conv_a_b[l],
             'conv_a_ln_g': conv_a_ln_g[l], 'conv_a_ln_b': conv_a_ln_b[l], 'conv_b_w': conv_b_w[l],
             'w_out': w_out[l], 'ln1_g': ln1_g[l], 'ln1_b': ln1_b[l],
             'w_group': w_group[l], 'b_group': b_group[l], 'w_router': w_router[l], 'b_router': b_router[l],
             'w_gate': w_gate[l], 'w_up': w_up[l], 'w_down': w_down[l], 'ln2_g': ln2_g[l], 'ln2_b': ln2_b[l]}
        h_p, a_p, b_p, k_p, v_p = layer_forward(h_p, zero_a, zero_b, attend_p, p)
        attend_s = functools.partial(moba_sample, cache_k=cache_k, cache_v=cache_v, layer=l,
                                     page_table=page_table, rel_bias=rel_bias)
        h_s, a_s, b_s, k_s, v_s = layer_forward(h_s, state_conv_a[l], state_conv_b[l], attend_s, p)
        kp_l.append(rows_to_pages(k_p))
        vp_l.append(rows_to_pages(v_p))
        ap_l.append(a_p)
        bp_l.append(b_p)
        ks_l.append(k_s.transpose(0, 2, 1, 3))
        vs_l.append(v_s.transpose(0, 2, 1, 3))
        as_l.append(a_s)
        bs_l.append(b_s)
    return (h_p, h_s, jnp.stack(kp_l), jnp.stack(vp_l), jnp.stack(ap_l), jnp.stack(bp_l),
            jnp.stack(ks_l), jnp.stack(vs_l), jnp.stack(as_l), jnp.stack(bs_l))
```

```python
import functools
import math

import numpy as np
import jax
import jax.numpy as jnp
from jax import lax
from jax.experimental import pallas as pl
from jax.experimental.pallas import tpu as pltpu

D_MODEL = 1024
HEAD_DIM = 64
D_CONF = 256
D_SCONV = 256
D_ATT = 512
N_HEADS = 8
CONF_WIDTH = 31
SCONV_WIDTH = 3
PAGE = 128
BLK = 256
TOPK = 3
N_BUCKETS = 32
MAX_DISTANCE = 128
N_GROUPS = 4
EPG = 4
N_EXPERTS = 16
D_EXPERT = 256
DEPTH = 4
ALPHA = (2 * DEPTH) ** 0.25
LN_EPS = 1e-5
NEG = -1e30
LOWEST = -3e38
ATT_SCALE = HEAD_DIM ** -0.5
D_IN = 2 * D_CONF + 3 * D_SCONV + 3 * D_ATT

F32 = jnp.float32
BF16 = jnp.bfloat16
HIGHEST = lax.Precision.HIGHEST

VMEM_LIMIT = 56 * 1024 * 1024


def _cparams(sem):
    return pltpu.CompilerParams(dimension_semantics=sem, vmem_limit_bytes=VMEM_LIMIT)


def _ln(x, g, b):
    mu = jnp.mean(x, axis=-1, keepdims=True)
    xc = x - mu
    var = jnp.mean(xc * xc, axis=-1, keepdims=True)
    return xc * lax.rsqrt(var + LN_EPS) * g + b


def _silu(x):
    return x * jax.nn.sigmoid(x)


def _dot(a, b):
    return jnp.dot(a, b, preferred_element_type=F32)


def _t5_bucket_np(dist):
    n = np.maximum(dist, 0)
    max_exact = N_BUCKETS // 2
    nf = np.maximum(n, 1).astype(np.float32)
    large = max_exact + (np.log(nf / np.float32(max_exact)) / np.float32(math.log(MAX_DISTANCE / max_exact))
                         * np.float32(N_BUCKETS - max_exact)).astype(np.int32)
    large = np.minimum(large, N_BUCKETS - 1)
    return np.where(n < max_exact, n, large).astype(np.int32)


def _bias_kernel(rb_ref, idx0_ref, idx1_ref, idxs_ref, btab_ref, stab_ref):
    h = pl.program_id(0)
    idx0 = idx0_ref[...]
    idx1 = idx1_ref[...]
    idxs = idxs_ref[...]
    t0 = jnp.zeros(idx0.shape, F32)
    t1 = jnp.zeros(idx1.shape, F32)
    ts = jnp.zeros(idxs.shape, F32)
    for bkt in range(N_BUCKETS):
        val = rb_ref[bkt, h]
        t0 = jnp.where(idx0 == bkt, val, t0)
        t1 = jnp.where(idx1 == bkt, val, t1)
        ts = jnp.where(idxs == bkt, val, ts)
    btab_ref[0, 0] = t0
    btab_ref[0, 1] = t1
    stab_ref[0] = ts


def _bias_tables(rel_bias):
    r = np.arange(BLK)[:, None]
    c = np.arange(BLK)[None, :]
    idx0 = _t5_bucket_np(r - c)
    idx1 = _t5_bucket_np(BLK + r - c)
    idxs = np.broadcast_to(_t5_bucket_np(BLK - np.arange(BLK))[:, None], (BLK, 128))
    return pl.pallas_call(
        _bias_kernel,
        grid=(N_HEADS,),
        in_specs=[pl.BlockSpec(memory_space=pltpu.SMEM),
                  pl.BlockSpec((BLK, BLK), lambda h: (0, 0)),
                  pl.BlockSpec((BLK, BLK), lambda h: (0, 0)),
                  pl.BlockSpec((BLK, 128), lambda h: (0, 0))],
        out_specs=[pl.BlockSpec((1, 2, BLK, BLK), lambda h: (h, 0, 0, 0)),
                   pl.BlockSpec((1, BLK, 128), lambda h: (h, 0, 0))],
        out_shape=[jax.ShapeDtypeStruct((N_HEADS, 2, BLK, BLK), F32),
                   jax.ShapeDtypeStruct((N_HEADS, BLK, 128), F32)],
        compiler_params=_cparams(("arbitrary",)),
        name="bias_tables",
    )(rel_bias, jnp.asarray(idx0), jnp.asarray(idx1), jnp.asarray(idxs))


TM1 = BLK


def _inproj_kernel(*refs, apply_ln):
    if apply_ln:
        (x_ref, lng_ref, lnb_ref, w_ref, caw_ref, cab_ref, calg_ref, calb_ref, cbw_ref,
         h_ref, q_ref, kt_ref, vb_ref, kpg_ref, vpg_ref, yab_ref, ksum_ref, newa_ref, newb_ref,
         ua_ref, ub_ref) = refs
    else:
        (x_ref, w_ref, caw_ref, cab_ref, calg_ref, calb_ref, cbw_ref,
         q_ref, kt_ref, vb_ref, kpg_ref, vpg_ref, yab_ref, ksum_ref, newa_ref, newb_ref,
         ua_ref, ub_ref) = refs
    ti = pl.program_id(1)
    last = pl.num_programs(1) - 1
    x = x_ref[0]
    if apply_ln:
        x = _ln(x, lng_ref[...], lnb_ref[...])
        h_ref[0] = x
    xb = x.astype(BF16)

    def seg(a, b):
        return _dot(xb, w_ref[:, a:b])

    @pl.when(ti == 0)
    def _():
        ua_ref[0:32, :] = jnp.zeros((32, D_CONF), F32)
        ub_ref[0:8, :] = jnp.zeros((8, D_SCONV), F32)

    u = seg(0, 256) * jax.nn.sigmoid(seg(256, 512))
    ua_ref[32:32 + TM1, :] = u
    acc = jnp.zeros((TM1, D_CONF), F32)
    for j in range(CONF_WIDTH):
        acc = acc + caw_ref[j:j + 1, :] * ua_ref[2 + j:2 + j + TM1, :]
    y_a = _silu(_ln(acc + cab_ref[...], calg_ref[...], calb_ref[...]))
    yab_ref[0, :, 0:256] = y_a.astype(BF16)

    ch = seg(768, 1024) * seg(1024, 1280)
    ub_ref[8:8 + TM1, :] = ch
    accb = jnp.zeros((TM1, D_SCONV), F32)
    for j in range(SCONV_WIDTH):
        accb = accb + cbw_ref[j:j + 1, :] * ub_ref[6 + j:6 + j + TM1, :]
    yab_ref[0, :, 256:512] = (seg(512, 768) * accb).astype(BF16)

    @pl.when(ti == last)
    def _():
        newa_ref[0] = ua_ref[TM1 + 2:TM1 + 32, :]
        newb_ref[0] = ub_ref[TM1 + 6:TM1 + 8, :]

    ua_ref[0:32, :] = ua_ref[TM1:TM1 + 32, :]
    ub_ref[0:8, :] = ub_ref[TM1:TM1 + 8, :]

    q_ref[0] = seg(1280, 1792) * ATT_SCALE
    k = seg(1792, 2304)
    v = seg(2304, 2816)
    kt_ref[0, 0] = k.T.astype(BF16)
    vb_ref[0] = v.astype(BF16)
    ksum_ref[0, 0] = jnp.sum(k, axis=0, keepdims=True)
    for p in range(TM1 // PAGE):
        for hd in range(N_HEADS):
            kpg_ref[0, p, hd] = k[p * PAGE:(p + 1) * PAGE, hd * HEAD_DIM:(hd + 1) * HEAD_DIM]
            vpg_ref[0, p, hd] = v[p * PAGE:(p + 1) * PAGE, hd * HEAD_DIM:(hd + 1) * HEAD_DIM]


def _inproj(x, ln_g, ln_b, w_in, caw, cab, calg, calb, cbw, apply_ln):
    bsz, t, _ = x.shape
    nt = t // TM1
    full2 = lambda b, i: (0, 0)
    in_specs = [pl.BlockSpec((1, TM1, D_MODEL), lambda b, i: (b, i, 0))]
    args = [x]
    if apply_ln:
        in_specs += [pl.BlockSpec((1, D_MODEL), full2), pl.BlockSpec((1, D_MODEL), full2)]
        args += [ln_g, ln_b]
    in_specs += [pl.BlockSpec((D_MODEL, D_IN), full2),
                 pl.BlockSpec((CONF_WIDTH, D_CONF), full2),
                 pl.BlockSpec((1, D_CONF), full2), pl.BlockSpec((1, D_CONF), full2),
                 pl.BlockSpec((1, D_CONF), full2),
                 pl.BlockSpec((SCONV_WIDTH, D_SCONV), full2)]
    args += [w_in, caw, cab, calg, calb, cbw]
    out_specs, out_shape = [], []
    if apply_ln:
        out_specs.append(pl.BlockSpec((1, TM1, D_MODEL), lambda b, i: (b, i, 0)))
        out_shape.append(jax.ShapeDtypeStruct((bsz, t, D_MODEL), F32))
    out_specs += [
        pl.BlockSpec((1, TM1, D_ATT), lambda b, i: (b, i, 0)),
        pl.BlockSpec((1, 1, D_ATT, TM1), lambda b, i: (b, i, 0, 0)),
        pl.BlockSpec((1, TM1, D_ATT), lambda b, i: (b, i, 0)),
        pl.BlockSpec((1, TM1 // PAGE, N_HEADS, PAGE, HEAD_DIM), lambda b, i: (b, i, 0, 0, 0)),
        pl.BlockSpec((1, TM1 // PAGE, N_HEADS, PAGE, HEAD_DIM), lambda b, i: (b, i, 0, 0, 0)),
        pl.BlockSpec((1, TM1, 512), lambda b, i: (b, i, 0)),
        pl.BlockSpec((1, 1, 1, D_ATT), lambda b, i: (b, i, 0, 0)),
        pl.BlockSpec((1, CONF_WIDTH - 1, D_CONF), lambda b, i: (b, 0, 0)),
        pl.BlockSpec((1, SCONV_WIDTH - 1, D_SCONV), lambda b, i: (b, 0, 0)),
    ]
    out_shape += [
        jax.ShapeDtypeStruct((bsz, t, D_ATT), F32),
        jax.ShapeDtypeStruct((bsz, nt, D_ATT, TM1), BF16),
        jax.ShapeDtypeStruct((bsz, t, D_ATT), BF16),
        jax.ShapeDtypeStruct((bsz, t // PAGE, N_HEADS, PAGE, HEAD_DIM), F32),
        jax.ShapeDtypeStruct((bsz, t // PAGE, N_HEADS, PAGE, HEAD_DIM), F32),
        jax.ShapeDtypeStruct((bsz, t, 512), BF16),
        jax.ShapeDtypeStruct((bsz, nt, 1, D_ATT), F32),
        jax.ShapeDtypeStruct((bsz, CONF_WIDTH - 1, D_CONF), F32),
        jax.ShapeDtypeStruct((bsz, SCONV_WIDTH - 1, D_SCONV), F32),
    ]
    return pl.pallas_call(
        functools.partial(_inproj_kernel, apply_ln=apply_ln),
        grid=(bsz, nt),
        in_specs=in_specs, out_specs=out_specs, out_shape=out_shape,
        scratch_shapes=[pltpu.VMEM((32 + TM1, D_CONF), F32), pltpu.VMEM((8 + TM1, D_SCONV), F32)],
        compiler_params=_cparams(("arbitrary", "arbitrary")),
        name="inproj_conv",
    )(*args)


def _top3_mask(gate, colf, valid):
    g = jnp.where(valid, gate, NEG)
    sel = jnp.zeros(gate.shape, F32)
    for _ in range(TOPK):
        mx = jnp.max(g, axis=1, keepdims=True)
        idx = jnp.min(jnp.where(g == mx, colf, 1e9), axis=1, keepdims=True)
        hit = colf == idx
        sel = jnp.where(hit, 1.0, sel)
        g = jnp.where(hit, LOWEST, g)
    return jnp.where(valid, sel, 0.0)


def _attn_kernel(rb_ref, q_ref, kt_ref, v_ref, ksum_ref, btab_ref, o_ref, m_ref, l_ref, acc_ref):
    hp = pl.program_id(1)
    i = pl.program_id(2)
    qf = q_ref[0]
    lane = lax.broadcasted_iota(jnp.int32, (BLK, 128), 1)
    col = lax.broadcasted_iota(jnp.int32, (BLK, 16), 1)
    colf = col.astype(F32)
    row2 = lax.broadcasted_iota(jnp.int32, (BLK, BLK), 0)
    col2 = lax.broadcasted_iota(jnp.int32, (BLK, BLK), 1)
    ksum = ksum_ref[0]

    def update(hh, s, vblk):
        m_old = m_ref[hh]
        m_new = jnp.maximum(m_old, jnp.max(s, axis=1, keepdims=True))
        alpha = jnp.exp(m_old - m_new)
        p = jnp.exp(s - m_new)
        l_ref[hh] = alpha * l_ref[hh] + jnp.sum(p, axis=1, keepdims=True)
        acc_ref[hh] = alpha * acc_ref[hh] + _dot(p.astype(BF16), vblk)
        m_ref[hh] = m_new

    outs = []
    for hh in range(2):
        head = hp * 2 + hh
        inh = (lane >= HEAD_DIM * hh) & (lane < HEAD_DIM * (hh + 1))
        qm = jnp.where(inh, qf, 0.0)
        qb = qm.astype(BF16)
        gate = lax.dot_general(qm, ksum, (((1,), (1,)), ((), ())), precision=HIGHEST,
                               preferred_element_type=F32)
        sel = _top3_mask(gate, colf, col < i)
        c31 = rb_ref[N_BUCKETS - 1, head]

        m_ref[hh] = jnp.full((BLK, 1), LOWEST, F32)
        l_ref[hh] = jnp.zeros((BLK, 1), F32)
        acc_ref[hh] = jnp.zeros((BLK, 128), F32)

        def selcol(n):
            return jnp.sum(jnp.where(col == n, sel, 0.0), axis=1, keepdims=True)

        def far(n, carry):
            off = pl.multiple_of(n * BLK, BLK)
            s = _dot(qb, kt_ref[0, n]) + c31
            s = jnp.where(selcol(n) > 0.5, s, NEG)
            update(hh, s, v_ref[0, pl.ds(off, BLK), :])
            return carry

        lax.fori_loop(0, jnp.maximum(i - 1, 0), far, 0)

        @pl.when(i >= 1)
        def _():
            n = i - 1
            off = pl.multiple_of(n * BLK, BLK)
            s = _dot(qb, kt_ref[0, n]) + btab_ref[hh, 1]
            s = jnp.where(selcol(n) > 0.5, s, NEG)
            update(hh, s, v_ref[0, pl.ds(off, BLK), :])

        off = pl.multiple_of(i * BLK, BLK)
        s = _dot(qb, kt_ref[0, i]) + btab_ref[hh, 0]
        s = jnp.where(row2 >= col2, s, NEG)
        update(hh, s, v_ref[0, pl.ds(off, BLK), :])
        outs.append(acc_ref[hh] / l_ref[hh])

    o_ref[0] = jnp.where(lane < HEAD_DIM, outs[0], outs[1]).astype(BF16)


def _attention(rel_bias, q, kt, vb, ksum, btab):
    bsz, t, _ = q.shape
    nb = t // BLK
    return pl.pallas_call(
        _attn_kernel,
        grid=(bsz, D_ATT // 128, nb),
        in_specs=[pl.BlockSpec(memory_space=pltpu.SMEM),
                  pl.BlockSpec((1, BLK, 128), lambda b, hp, i: (b, i, hp)),
                  pl.BlockSpec((1, nb, 128, BLK), lambda b, hp, i: (b, 0, hp, 0)),
                  pl.BlockSpec((1, t, 128), lambda b, hp, i: (b, 0, hp)),
                  pl.BlockSpec((1, nb, 128), lambda b, hp, i: (b, 0, hp)),
                  pl.BlockSpec((2, 2, BLK, BLK), lambda b, hp, i: (hp, 0, 0, 0))],
        out_specs=pl.BlockSpec((1, BLK, 128), lambda b, hp, i: (b, i, hp)),
        out_shape=jax.ShapeDtypeStruct((bsz, t, D_ATT), BF16),
        scratch_shapes=[pltpu.VMEM((2, BLK, 1), F32), pltpu.VMEM((2, BLK, 1), F32),
                        pltpu.VMEM((2, BLK, 128), F32)],
        compiler_params=_cparams(("arbitrary", "arbitrary", "arbitrary")),
        name="moba_attention",
    )(rel_bias, q, kt, vb, ksum, btab)


def _outproj_kernel(x_ref, yab_ref, yc_ref, w_ref, g_ref, b_ref, wr_ref, br_ref, x1_ref, comb_ref):
    mix = _dot(yab_ref[...], w_ref[0:512, :]) + _dot(yc_ref[...], w_ref[512:1024, :])
    x1 = _ln(ALPHA * x_ref[...] + mix, g_ref[...], b_ref[...])
    x1_ref[...] = x1
    lg = jnp.dot(x1, wr_ref[...], precision=HIGHEST, preferred_element_type=F32) + br_ref[...]
    lanef = lax.broadcasted_iota(jnp.int32, lg.shape, 1).astype(F32)
    isg = (lanef >= N_EXPERTS) & (lanef < N_EXPERTS + N_GROUPS)
    gl = jnp.where(isg, lg, LOWEST)
    gexp = jnp.where(isg, jnp.exp(gl - jnp.max(gl, axis=1, keepdims=True)), 0.0)
    gprob = gexp / jnp.sum(gexp, axis=1, keepdims=True)
    g_w = jnp.max(gprob, axis=1, keepdims=True)
    g_idx = jnp.min(jnp.where(isg & (gprob == g_w), lanef, 1e9), axis=1, keepdims=True) - N_EXPERTS
    ise = (lanef >= EPG * g_idx) & (lanef < EPG * g_idx + EPG)
    el = jnp.where(ise, lg, LOWEST)
    eexp = jnp.where(ise, jnp.exp(el - jnp.max(el, axis=1, keepdims=True)), 0.0)
    eprob = eexp / jnp.sum(eexp, axis=1, keepdims=True)
    p1 = jnp.max(eprob, axis=1, keepdims=True)
    i1 = jnp.min(jnp.where(ise & (eprob == p1), lanef, 1e9), axis=1, keepdims=True)
    hit1 = lanef == i1
    rest = jnp.where(ise, jnp.where(hit1, -1.0, eprob), -1.0)
    p2 = jnp.max(rest, axis=1, keepdims=True)
    i2 = jnp.min(jnp.where(rest == p2, jnp.where(ise, jnp.where(hit1, 1e9, lanef), 1e9), 1e9),
                 axis=1, keepdims=True)
    hit2 = lanef == i2
    den = p1 + p2
    comb_ref[...] = jnp.where(hit1, g_w * (p1 / den), jnp.where(hit2, g_w * (p2 / den), 0.0))


def _outproj(x, yab, yc, w_out, g, b, wr, br, tm):
    n = x.shape[0]
    full = lambda i: (0, 0)
    return pl.pallas_call(
        _outproj_kernel,
        grid=(n // tm,),
        in_specs=[pl.BlockSpec((tm, D_MODEL), lambda i: (i, 0)),
                  pl.BlockSpec((tm, 512), lambda i: (i, 0)),
                  pl.BlockSpec((tm, 512), lambda i: (i, 0)),
                  pl.BlockSpec((D_MODEL, D_MODEL), full),
                  pl.BlockSpec((1, D_MODEL), full), pl.BlockSpec((1, D_MODEL), full),
                  pl.BlockSpec((D_MODEL, 128), full), pl.BlockSpec((1, 128), full)],
        out_specs=[pl.BlockSpec((tm, D_MODEL), lambda i: (i, 0)),
                   pl.BlockSpec((tm, 128), lambda i: (i, 0))],
        out_shape=[jax.ShapeDtypeStruct((n, D_MODEL), F32), jax.ShapeDtypeStruct((n, 128), F32)],
        compiler_params=_cparams(("arbitrary",)),
        name="outproj_ln_route",
    )(x, yab, yc, w_out, g, b, wr, br)


def _moe_kernel(x_ref, comb_ref, wgu_ref, wd_ref, g_ref, b_ref, o_ref, xb_ref, acc_ref):
    e = pl.program_id(1)

    @pl.when(e == 0)
    def _():
        xb_ref[...] = x_ref[...].astype(BF16)
        acc_ref[...] = jnp.zeros(acc_ref.shape, F32)

    gu = _dot(xb_ref[...], wgu_ref[0])
    hid = _silu(gu[:, 0:D_EXPERT]) * gu[:, D_EXPERT:2 * D_EXPERT]
    comb = comb_ref[...]
    lane = lax.broadcasted_iota(jnp.int32, comb.shape, 1)
    c = jnp.sum(jnp.where(lane == e, comb, 0.0), axis=1, keepdims=True)
    acc_ref[...] += _dot((hid * c).astype(BF16), wd_ref[0])

    @pl.when(e == pl.num_programs(1) - 1)
    def _():
        o_ref[...] = _ln(ALPHA * x_ref[...] + acc_ref[...], g_ref[...], b_ref[...])


def _moe(x1, comb, wgu, wd, g, b, tm):
    n = x1.shape[0]
    return pl.pallas_call(
        _moe_kernel,
        grid=(n // tm, N_EXPERTS),
        in_specs=[pl.BlockSpec((tm, D_MODEL), lambda i, e: (i, 0)),
                  pl.BlockSpec((tm, 128), lambda i, e: (i, 0)),
                  pl.BlockSpec((1, D_MODEL, 2 * D_EXPERT), lambda i, e: (e, 0, 0)),
                  pl.BlockSpec((1, D_EXPERT, D_MODEL), lambda i, e: (e, 0, 0)),
                  pl.BlockSpec((1, D_MODEL), lambda i, e: (0, 0)),
                  pl.BlockSpec((1, D_MODEL), lambda i, e: (0, 0))],
        out_specs=pl.BlockSpec((tm, D_MODEL), lambda i, e: (i, 0)),
        out_shape=jax.ShapeDtypeStruct((n, D_MODEL), F32),
        scratch_shapes=[pltpu.VMEM((tm, D_MODEL), BF16), pltpu.VMEM((tm, D_MODEL), F32)],
        compiler_params=_cparams(("arbitrary", "arbitrary")),
        name="moe_ffn_ln",
    )(x1, comb, wgu, wd, g, b)


def _sinproj_kernel(*refs, apply_ln):
    if apply_ln:
        (x_ref, lng_ref, lnb_ref, w_ref, caw_ref, cab_ref, calg_ref, calb_ref, cbw_ref, sta_ref, stb_ref,
         h_ref, q_ref, k_ref, v_ref, yab_ref, u_ref, ch_ref) = refs
    else:
        (x_ref, w_ref, caw_ref, cab_ref, calg_ref, calb_ref, cbw_ref, sta_ref, stb_ref,
         q_ref, k_ref, v_ref, yab_ref, u_ref, ch_ref) = refs
    x = x_ref[...]
    if apply_ln:
        x = _ln(x, lng_ref[...], lnb_ref[...])
        h_ref[...] = x
    xb = x.astype(BF16)

    def seg(a, b):
        return _dot(xb, w_ref[:, a:b])

    u = seg(0, 256) * jax.nn.sigmoid(seg(256, 512))
    u_ref[...] = u
    acc = caw_ref[CONF_WIDTH - 1:CONF_WIDTH, :] * u
    for j in range(CONF_WIDTH - 1):
        acc = acc + caw_ref[j:j + 1, :] * sta_ref[0, j]
    y_a = _silu(_ln(acc + cab_ref[...], calg_ref[...], calb_ref[...]))
    yab_ref[:, 0:256] = y_a.astype(BF16)

    ch = seg(768, 1024) * seg(1024, 1280)
    ch_ref[...] = ch
    accb = cbw_ref[SCONV_WIDTH - 1:SCONV_WIDTH, :] * ch
    for j in range(SCONV_WIDTH - 1):
        accb = accb + cbw_ref[j:j + 1, :] * stb_ref[0, j]
    yab_ref[:, 256:512] = (seg(512, 768) * accb).astype(BF16)

    q_ref[...] = seg(1280, 1792) * ATT_SCALE
    k_ref[...] = seg(1792, 2304)
    v_ref[...] = seg(2304, 2816)


def _sinproj(x, ln_g, ln_b, w_in, caw, cab, calg, calb, cbw, sta_t, stb_t, layer, apply_ln):
    n = x.shape[0]
    full = lambda i: (0, 0)
    in_specs = [pl.BlockSpec((n, D_MODEL), full)]
    args = [x]
    if apply_ln:
        in_specs += [pl.BlockSpec((1, D_MODEL), full), pl.BlockSpec((1, D_MODEL), full)]
        args += [ln_g, ln_b]
    in_specs += [pl.BlockSpec((D_MODEL, D_IN), full),
                 pl.BlockSpec((CONF_WIDTH, D_CONF), full),
                 pl.BlockSpec((1, D_CONF), full), pl.BlockSpec((1, D_CONF), full), pl.BlockSpec((1, D_CONF), full),
                 pl.BlockSpec((SCONV_WIDTH, D_SCONV), full),
                 pl.BlockSpec((1, CONF_WIDTH - 1, n, D_CONF), lambda i: (layer, 0, 0, 0)),
                 pl.BlockSpec((1, SCONV_WIDTH - 1, n, D_SCONV), lambda i: (layer, 0, 0, 0))]
    args += [w_in, caw, cab, calg, calb, cbw, sta_t, stb_t]
    out_specs, out_shape = [], []
    if apply_ln:
        out_specs.append(pl.BlockSpec((n, D_MODEL), full))
        out_shape.append(jax.ShapeDtypeStruct((n, D_MODEL), F32))
    out_specs += [pl.BlockSpec((n, D_ATT), full)] * 3 + [pl.BlockSpec((n, 512), full),
                                                        pl.BlockSpec((n, D_CONF), full),
                                                        pl.BlockSpec((n, D_SCONV), full)]
    out_shape += [jax.ShapeDtypeStruct((n, D_ATT), F32)] * 3 + [jax.ShapeDtypeStruct((n, 512), BF16),
                                                               jax.ShapeDtypeStruct((n, D_CONF), F32),
                                                               jax.ShapeDtypeStruct((n, D_SCONV), F32)]
    return pl.pallas_call(
        functools.partial(_sinproj_kernel, apply_ln=apply_ln),
        grid=(1,),
        in_specs=in_specs, out_specs=out_specs, out_shape=out_shape,
        compiler_params=_cparams(("arbitrary",)),
        name="sample_inproj_conv",
    )(*args)


N_PAGES = 16
N_PAST_BLK = N_PAGES * PAGE // BLK


def _sgate_kernel(pt_ref, q_ref, ptv_ref, *refs):
    krefs = refs[:N_PAGES]
    out_ref = refs[N_PAGES]
    q8 = q_ref[0]
    lane = lax.broadcasted_iota(jnp.int32, (N_HEADS, 128), 1)
    lanef = lane.astype(F32)
    psum = []
    for j in range(N_PAGES):
        rows = [jnp.sum(krefs[j][0, 0, hd], axis=0, keepdims=True) for hd in range(N_HEADS)]
        psum.append(jnp.concatenate(rows, axis=0))
    g = jnp.zeros((N_HEADS, 128), F32)
    for n in range(N_PAST_BLK):
        bsum = psum[2 * n] + psum[2 * n + 1]
        gn = jnp.sum(bsum * q8, axis=1, keepdims=True)
        g = jnp.where(lane == n, gn, g)
    g = jnp.where(lane < N_PAST_BLK, g, LOWEST)
    ptf = ptv_ref[0].astype(F32)
    l16 = lax.broadcasted_iota(jnp.int32, (N_HEADS, N_PAGES), 1).astype(F32)
    res = jnp.zeros((N_HEADS, 128), F32)
    for r in range(TOPK):
        mx = jnp.max(g, axis=1, keepdims=True)
        idx = jnp.min(jnp.where(g == mx, lanef, 1e9), axis=1, keepdims=True)
        g = jnp.where(lanef == idx, LOWEST, g)
        pg0 = jnp.sum(jnp.where(l16 == 2.0 * idx, ptf, 0.0), axis=1, keepdims=True)
        pg1 = jnp.sum(jnp.where(l16 == 2.0 * idx + 1.0, ptf, 0.0), axis=1, keepdims=True)
        res = jnp.where(lane == 2 * r, pg0, res)
        res = jnp.where(lane == 2 * r + 1, pg1, res)
        res = jnp.where(lane == 8 + r, idx, res)
    out_ref[0] = res.astype(jnp.int32)


def _sgate(page_table, q8, cache_k, layer):
    nseq = page_table.shape[0]
    ptv = page_table.reshape(nseq, 1, N_PAGES)

    def kspec(j):
        return pl.BlockSpec((1, 1, N_HEADS, PAGE, HEAD_DIM), lambda b, pt: (layer, pt[b * N_PAGES + j], 0, 0, 0))

    grid_spec = pltpu.PrefetchScalarGridSpec(
        num_scalar_prefetch=1,
        grid=(nseq,),
        in_specs=[pl.BlockSpec((1, N_HEADS, HEAD_DIM), lambda b, pt: (b, 0, 0)),
                  pl.BlockSpec((1, 1, N_PAGES), lambda b, pt: (b, 0, 0))]
                 + [kspec(j) for j in range(N_PAGES)],
        out_specs=pl.BlockSpec((1, N_HEADS, 128), lambda b, pt: (b, 0, 0)),
    )
    return pl.pallas_call(
        _sgate_kernel,
        grid_spec=grid_spec,
        out_shape=jax.ShapeDtypeStruct((nseq, N_HEADS, 128), jnp.int32),
        compiler_params=_cparams(("arbitrary",)),
        name="sample_block_gate",
    )(page_table.reshape(-1), q8, ptv, *([cache_k] * N_PAGES))


N_SEL_PAGES = TOPK * BLK // PAGE
SEL_STRIDE = 16


def _sattn_kernel(sel_ref, rb_ref, q_ref, kn_ref, vn_ref, stab_ref, *refs):
    np_ = N_HEADS * N_SEL_PAGES
    krefs = refs[:np_]
    vrefs = refs[np_:2 * np_]
    o_ref = refs[2 * np_]
    b = pl.program_id(0)
    for hd in range(N_HEADS):
        qh = q_ref[0, hd:hd + 1, :]
        base = (b * N_HEADS + hd) * SEL_STRIDE
        c31 = rb_ref[N_BUCKETS - 1, hd]
        scores = []
        for s in range(N_SEL_PAGES):
            kp = krefs[hd * N_SEL_PAGES + s][0, 0, 0]
            sc = jnp.sum(kp * qh, axis=1, keepdims=True)
            blk = sel_ref[base + 8 + s // 2]
            tab = stab_ref[hd, (s % 2) * PAGE:(s % 2 + 1) * PAGE, 0:1]
            scores.append(sc + jnp.where(blk == N_PAST_BLK - 1, tab, c31))
        s_self = jnp.sum(kn_ref[0, hd:hd + 1, :] * qh, axis=1, keepdims=True) + rb_ref[0, hd]
        m = s_self
        for sc in scores:
            m = jnp.maximum(m, jnp.max(sc, axis=0, keepdims=True))
        e_self = jnp.exp(s_self - m)
        den = e_self
        num = e_self * vn_ref[0, hd:hd + 1, :]
        for s in range(N_SEL_PAGES):
            e = jnp.exp(scores[s] - m)
            den = den + jnp.sum(e, axis=0, keepdims=True)
            num = num + jnp.sum(e * vrefs[hd * N_SEL_PAGES + s][0, 0, 0], axis=0, keepdims=True)
        o_ref[0, hd:hd + 1, :] = num / den


def _sattn(sel, rel_bias, q8, kn8, vn8, stab, cache_k, cache_v, layer):
    nseq = q8.shape[0]

    def pspec(hd, s):
        return pl.BlockSpec((1, 1, 1, PAGE, HEAD_DIM),
                            lambda b, sl: (layer, sl[(b * N_HEADS + hd) * SEL_STRIDE + s], hd, 0, 0))

    def page_specs():
        return [pspec(hd, s) for hd in range(N_HEADS) for s in range(N_SEL_PAGES)]

    def row():
        return pl.BlockSpec((1, N_HEADS, HEAD_DIM), lambda b, sl: (b, 0, 0))

    grid_spec = pltpu.PrefetchScalarGridSpec(
        num_scalar_prefetch=1,
        grid=(nseq,),
        in_specs=[pl.BlockSpec(memory_space=pltpu.SMEM), row(), row(), row(),
                  pl.BlockSpec((N_HEADS, BLK, 128), lambda b, sl: (0, 0, 0))] + page_specs() + page_specs(),
        out_specs=row(),
    )
    npg = N_HEADS * N_SEL_PAGES
    return pl.pallas_call(
        _sattn_kernel,
        grid_spec=grid_spec,
        out_shape=jax.ShapeDtypeStruct((nseq, N_HEADS, HEAD_DIM), F32),
        compiler_params=_cparams(("arbitrary",)),
        name="sample_attention",
    )(sel.reshape(-1), rel_bias, q8, kn8, vn8, stab, *([cache_k] * npg), *([cache_v] * npg))


TM_OUT = 512
TM_MOE = 1024


def kernel(x_prompt, x_sample, cache_k, cache_v, state_conv_a, state_conv_b, page_table, ln_in_g, ln_in_b, w_in, conv_a_w, conv_a_b, conv_a_ln_g, conv_a_ln_b, conv_b_w, rel_bias, w_out, ln1_g, ln1_b, w_group, b_group, w_router, b_router, w_gate, w_up, w_down, ln2_g, ln2_b):
    bsz, t, _ = x_prompt.shape
    nseq = x_sample.shape[0]
    n_p = bsz * t
    row = lambda a: a.reshape(1, -1)

    btab, stab = _bias_tables(rel_bias)
    sta_t = state_conv_a.transpose(0, 2, 1, 3)
    stb_t = state_conv_b.transpose(0, 2, 1, 3)
    x_s = x_sample.reshape(nseq, D_MODEL)

    h_p = x_prompt
    h_s = x_s
    kp_l, vp_l, ap_l, bp_l, ks_l, vs_l, us_l, cs_l = [], [], [], [], [], [], [], []
    for l in range(DEPTH):
        first = l == 0
        w_in_b = w_in[l].astype(BF16)
        w_out_b = w_out[l].astype(BF16)
        wgu = jnp.concatenate([w_gate[l], w_up[l]], axis=-1).astype(BF16)
        wd = w_down[l].astype(BF16)
        wr = jnp.pad(jnp.concatenate([w_router[l], w_group[l]], axis=-1), ((0, 0), (0, 128 - N_EXPERTS - N_GROUPS)))
        br = jnp.pad(jnp.concatenate([b_router[l], b_group[l]]), (0, 128 - N_EXPERTS - N_GROUPS)).reshape(1, 128)
        conv_args = (conv_a_w[l], row(conv_a_b[l]), row(conv_a_ln_g[l]), row(conv_a_ln_b[l]), conv_b_w[l])

        outs = _inproj(h_p, row(ln_in_g), row(ln_in_b), w_in_b, *conv_args, apply_ln=first)
        if first:
            h_p, outs = outs[0], outs[1:]
        q, kt, vb, kpg, vpg, yab, ksum, new_a, new_b = outs
        yc = _attention(rel_bias, q, kt, vb, ksum.reshape(bsz, t // BLK, D_ATT), btab)
        x1, comb = _outproj(h_p.reshape(n_p, D_MODEL), yab.reshape(n_p, 512), yc.reshape(n_p, D_ATT),
                            w_out_b, row(ln1_g[l]), row(ln1_b[l]), wr, br, TM_OUT)
        h_p = _moe(x1, comb, wgu, wd, row(ln2_g[l]), row(ln2_b[l]), TM_MOE).reshape(bsz, t, D_MODEL)
        kp_l.append(kpg)
        vp_l.append(vpg)
        ap_l.append(new_a)
        bp_l.append(new_b)

        souts = _sinproj(h_s, row(ln_in_g), row(ln_in_b), w_in_b, *conv_args, sta_t, stb_t, l, apply_ln=first)
        if first:
            h_s, souts = souts[0], souts[1:]
        q_s, k_s, v_s, yab_s, u_s, ch_s = souts
        q8 = q_s.reshape(nseq, N_HEADS, HEAD_DIM)
        k8 = k_s.reshape(nseq, N_HEADS, HEAD_DIM)
        v8 = v_s.reshape(nseq, N_HEADS, HEAD_DIM)
        sel = _sgate(page_table, q8, cache_k, l)[:, :, :SEL_STRIDE]
        yc_s = _sattn(sel, rel_bias, q8, k8, v8, stab, cache_k, cache_v, l)
        x1_s, comb_s = _outproj(h_s, yab_s, yc_s.reshape(nseq, D_ATT).astype(BF16),
                                w_out_b, row(ln1_g[l]), row(ln1_b[l]), wr, br, nseq)
        h_s = _moe(x1_s, comb_s, wgu, wd, row(ln2_g[l]), row(ln2_b[l]), nseq)
        ks_l.append(k8.reshape(nseq, N_HEADS, 1, HEAD_DIM))
        vs_l.append(v8.reshape(nseq, N_HEADS, 1, HEAD_DIM))
        us_l.append(u_s)
        cs_l.append(ch_s)

    new_a_s = jnp.concatenate([state_conv_a[:, :, 1:], jnp.stack(us_l)[:, :, None, :]], axis=2)
    new_b_s = jnp.concatenate([state_conv_b[:, :, 1:], jnp.stack(cs_l)[:, :, None, :]], axis=2)
    return (h_p, h_s.reshape(nseq, 1, D_MODEL), jnp.stack(kp_l), jnp.stack(vp_l), jnp.stack(ap_l), jnp.stack(bp_l),
            jnp.stack(ks_l), jnp.stack(vs_l), new_a_s, new_b_s)
```

```python
import functools
import math

import numpy as np
import jax
import jax.numpy as jnp
from jax import lax
from jax.experimental import pallas as pl
from jax.experimental.pallas import tpu as pltpu

D_MODEL = 1024
HEAD_DIM = 64
D_CONF = 256
D_SCONV = 256
D_ATT = 512
N_HEADS = 8
CONF_WIDTH = 31
SCONV_WIDTH = 3
PAGE = 128
BLK = 256
TOPK = 3
N_BUCKETS = 32
MAX_DISTANCE = 128
N_GROUPS = 4
EPG = 4
N_EXPERTS = 16
D_EXPERT = 256
DEPTH = 4
ALPHA = (2 * DEPTH) ** 0.25
LN_EPS = 1e-5
NEG = -1e30
LOWEST = -3e38
ATT_SCALE = HEAD_DIM ** -0.5
LOG2E = math.log2(math.e)
D_IN = 2 * D_CONF + 3 * D_SCONV + 3 * D_ATT
LANES = 128

F32 = jnp.float32
BF16 = jnp.bfloat16
HIGHEST = lax.Precision.HIGHEST

VMEM_LIMIT = 56 * 1024 * 1024


def _cparams(sem):
    return pltpu.CompilerParams(dimension_semantics=sem, vmem_limit_bytes=VMEM_LIMIT)


def _ln(x, g, b):
    mu = jnp.mean(x, axis=-1, keepdims=True)
    xc = x - mu
    var = jnp.mean(xc * xc, axis=-1, keepdims=True)
    return xc * lax.rsqrt(var + LN_EPS) * g + b


def _silu(x):
    return x * jax.nn.sigmoid(x)


def _dot(a, b):
    return jnp.dot(a, b, preferred_element_type=F32)


def _top3_rows(gate, rowf, valid):
    g = jnp.where(valid, gate, NEG)
    sel = jnp.zeros(gate.shape, F32)
    for _ in range(TOPK):
        mx = jnp.max(g, axis=0, keepdims=True)
        idx = jnp.min(jnp.where(g == mx, rowf, 1e9), axis=0, keepdims=True)
        hit = rowf == idx
        sel = jnp.where(hit, 1.0, sel)
        g = jnp.where(hit, LOWEST, g)
    return jnp.where(valid, sel, 0.0)


def _t5_bucket_np(dist):
    n = np.maximum(dist, 0)
    max_exact = N_BUCKETS // 2
    nf = np.maximum(n, 1).astype(np.float32)
    large = max_exact + (np.log(nf / np.float32(max_exact)) / np.float32(math.log(MAX_DISTANCE / max_exact))
                         * np.float32(N_BUCKETS - max_exact)).astype(np.int32)
    large = np.minimum(large, N_BUCKETS - 1)
    return np.where(n < max_exact, n, large).astype(np.int32)


def _bias_kernel(rb_ref, idx0_ref, idx1_ref, idxs_ref, btab_ref, stab_ref):
    h = pl.program_id(0)
    idx0 = idx0_ref[...]
    idx1 = idx1_ref[...]
    idxs = idxs_ref[...]
    t0 = jnp.zeros(idx0.shape, F32)
    t1 = jnp.zeros(idx1.shape, F32)
    ts = jnp.zeros(idxs.shape, F32)
    for bkt in range(N_BUCKETS):
        val = rb_ref[bkt, h]
        t0 = jnp.where(idx0 == bkt, val, t0)
        t1 = jnp.where(idx1 == bkt, val, t1)
        ts = jnp.where(idxs == bkt, val, ts)
    far = rb_ref[N_BUCKETS - 1, h]
    btab_ref[0, 0] = (t0 - far) * LOG2E
    btab_ref[0, 1] = (t1 - far) * LOG2E
    stab_ref[0] = ts


def _bias_tables(rel_bias):
    r = np.arange(BLK)[:, None]
    c = np.arange(BLK)[None, :]
    idx0 = _t5_bucket_np(c - r)
    idx1 = _t5_bucket_np(BLK + c - r)
    idxs = np.zeros((8, PAGE), np.int32)
    idxs[0:2] = _t5_bucket_np(BLK - np.arange(BLK)).reshape(2, PAGE)
    return pl.pallas_call(
        _bias_kernel,
        grid=(N_HEADS,),
        in_specs=[pl.BlockSpec(memory_space=pltpu.SMEM),
                  pl.BlockSpec((BLK, BLK), lambda h: (0, 0)),
                  pl.BlockSpec((BLK, BLK), lambda h: (0, 0)),
                  pl.BlockSpec((8, PAGE), lambda h: (0, 0))],
        out_specs=[pl.BlockSpec((1, 2, BLK, BLK), lambda h: (h, 0, 0, 0)),
                   pl.BlockSpec((1, 8, PAGE), lambda h: (h, 0, 0))],
        out_shape=[jax.ShapeDtypeStruct((N_HEADS, 2, BLK, BLK), F32),
                   jax.ShapeDtypeStruct((N_HEADS, 8, PAGE), F32)],
        compiler_params=_cparams(("arbitrary",)),
        name="bias_tables",
    )(rel_bias, jnp.asarray(idx0), jnp.asarray(idx1), jnp.asarray(idxs))


TM1 = BLK


def _inproj_kernel(*refs, apply_ln):
    if apply_ln:
        (x_ref, lng_ref, lnb_ref, w_ref, caw_ref, cab_ref, calg_ref, calb_ref, cbw_ref, _kin, _vin,
         h_ref, qt_ref, kb_ref, vt_ref, kpg_ref, vpg_ref, yab_ref, ksum_ref, newa_ref, newb_ref,
         ua_ref, ub_ref) = refs
    else:
        (x_ref, w_ref, caw_ref, cab_ref, calg_ref, calb_ref, cbw_ref, _kin, _vin,
         qt_ref, kb_ref, vt_ref, kpg_ref, vpg_ref, yab_ref, ksum_ref, newa_ref, newb_ref,
         ua_ref, ub_ref) = refs
    ti = pl.program_id(1)
    last = pl.num_programs(1) - 1
    x = x_ref[0]
    if apply_ln:
        x = _ln(x, lng_ref[...], lnb_ref[...])
        h_ref[0] = x
    xb = x.astype(BF16)

    def seg(a, b):
        return _dot(xb, w_ref[:, a:b])

    @pl.when(ti == 0)
    def _():
        ua_ref[0:32, :] = jnp.zeros((32, D_CONF), F32)
        ub_ref[0:8, :] = jnp.zeros((8, D_SCONV), F32)

    u = seg(0, 256) * jax.nn.sigmoid(seg(256, 512))
    ua_ref[32:32 + TM1, :] = u
    acc = jnp.zeros((TM1, D_CONF), F32)
    for j in range(CONF_WIDTH):
        acc = acc + caw_ref[j:j + 1, :] * ua_ref[2 + j:2 + j + TM1, :]
    y_a = _silu(_ln(acc + cab_ref[...], calg_ref[...], calb_ref[...]))
    yab_ref[0, :, 0:256] = y_a.astype(BF16)

    ch = seg(768, 1024) * seg(1024, 1280)
    ub_ref[8:8 + TM1, :] = ch
    accb = jnp.zeros((TM1, D_SCONV), F32)
    for j in range(SCONV_WIDTH):
        accb = accb + cbw_ref[j:j + 1, :] * ub_ref[6 + j:6 + j + TM1, :]
    yab_ref[0, :, 256:512] = (seg(512, 768) * accb).astype(BF16)

    @pl.when(ti == last)
    def _():
        newa_ref[0] = ua_ref[TM1 + 2:TM1 + 32, :]
        newb_ref[0] = ub_ref[TM1 + 6:TM1 + 8, :]

    ua_ref[0:32, :] = ua_ref[TM1:TM1 + 32, :]
    ub_ref[0:8, :] = ub_ref[TM1:TM1 + 8, :]

    qt_ref[0] = (seg(1280, 1792) * (ATT_SCALE * LOG2E)).T
    k = seg(1792, 2304)
    v = seg(2304, 2816)
    kt = k.T
    vt = v.T
    kb_ref[0] = k.astype(BF16)
    vt_ref[0, 0] = vt.astype(BF16)
    ksum_ref[0, 0] = jnp.sum(k, axis=0, keepdims=True)
    for p in range(TM1 // PAGE):
        for hd in range(N_HEADS):
            kpg_ref[0, 0, p, hd] = kt[hd * HEAD_DIM:(hd + 1) * HEAD_DIM, p * PAGE:(p + 1) * PAGE]
            vpg_ref[0, 0, p, hd] = vt[hd * HEAD_DIM:(hd + 1) * HEAD_DIM, p * PAGE:(p + 1) * PAGE]


def _inproj(x, ln_g, ln_b, w_in, caw, cab, calg, calb, cbw, kpages, vpages, layer, apply_ln):
    bsz, t, _ = x.shape
    nt = t // TM1
    full2 = lambda b, i: (0, 0)
    in_specs = [pl.BlockSpec((1, TM1, D_MODEL), lambda b, i: (b, i, 0))]
    args = [x]
    if apply_ln:
        in_specs += [pl.BlockSpec((1, D_MODEL), full2), pl.BlockSpec((1, D_MODEL), full2)]
        args += [ln_g, ln_b]
    in_specs += [pl.BlockSpec((D_MODEL, D_IN), full2),
                 pl.BlockSpec((CONF_WIDTH, D_CONF), full2),
                 pl.BlockSpec((1, D_CONF), full2), pl.BlockSpec((1, D_CONF), full2),
                 pl.BlockSpec((1, D_CONF), full2),
                 pl.BlockSpec((SCONV_WIDTH, D_SCONV), full2),
                 pl.BlockSpec(memory_space=pl.ANY), pl.BlockSpec(memory_space=pl.ANY)]
    args += [w_in, caw, cab, calg, calb, cbw, kpages, vpages]
    kpages_in = len(args) - 2
    n_extra = 1 if apply_ln else 0
    page_block = (1, 1, TM1 // PAGE, N_HEADS, HEAD_DIM, PAGE)
    page_map = lambda b, i: (layer, b, i, 0, 0, 0)
    out_specs, out_shape = [], []
    if apply_ln:
        out_specs.append(pl.BlockSpec((1, TM1, D_MODEL), lambda b, i: (b, i, 0)))
        out_shape.append(jax.ShapeDtypeStruct((bsz, t, D_MODEL), F32))
    out_specs += [
        pl.BlockSpec((1, D_ATT, TM1), lambda b, i: (b, 0, i)),
        pl.BlockSpec((1, TM1, D_ATT), lambda b, i: (b, i, 0)),
        pl.BlockSpec((1, 1, D_ATT, TM1), lambda b, i: (b, i, 0, 0)),
        pl.BlockSpec(page_block, page_map),
        pl.BlockSpec(page_block, page_map),
        pl.BlockSpec((1, TM1, 512), lambda b, i: (b, i, 0)),
        pl.BlockSpec((1, 1, 1, D_ATT), lambda b, i: (b, i, 0, 0)),
        pl.BlockSpec((1, CONF_WIDTH - 1, D_CONF), lambda b, i: (b, 0, 0)),
        pl.BlockSpec((1, SCONV_WIDTH - 1, D_SCONV), lambda b, i: (b, 0, 0)),
    ]
    out_shape += [
        jax.ShapeDtypeStruct((bsz, D_ATT, t), F32),
        jax.ShapeDtypeStruct((bsz, t, D_ATT), BF16),
        jax.ShapeDtypeStruct((bsz, nt, D_ATT, TM1), BF16),
        jax.ShapeDtypeStruct(kpages.shape, F32),
        jax.ShapeDtypeStruct(vpages.shape, F32),
        jax.ShapeDtypeStruct((bsz, t, 512), BF16),
        jax.ShapeDtypeStruct((bsz, nt, 1, D_ATT), F32),
        jax.ShapeDtypeStruct((bsz, CONF_WIDTH - 1, D_CONF), F32),
        jax.ShapeDtypeStruct((bsz, SCONV_WIDTH - 1, D_SCONV), F32),
    ]
    return pl.pallas_call(
        functools.partial(_inproj_kernel, apply_ln=apply_ln),
        grid=(bsz, nt),
        in_specs=in_specs, out_specs=out_specs, out_shape=out_shape,
        input_output_aliases={kpages_in: n_extra + 3, kpages_in + 1: n_extra + 4},
        scratch_shapes=[pltpu.VMEM((32 + TM1, D_CONF), F32), pltpu.VMEM((8 + TM1, D_SCONV), F32)],
        compiler_params=_cparams(("arbitrary", "arbitrary")),
        name="inproj_conv",
    )(*args)


def _attn_kernel(qt_ref, k_ref, vt_ref, ksum_ref, btab_ref, o_ref, sel_ref, acc_ref):
    i = pl.program_id(2)
    nb = ksum_ref.shape[1]
    qt = qt_ref[0]
    rowd = lax.broadcasted_iota(jnp.int32, (LANES, BLK), 0)
    rown = lax.broadcasted_iota(jnp.int32, (nb, BLK), 0)
    rownf = rown.astype(F32)
    rk = lax.broadcasted_iota(jnp.int32, (BLK, BLK), 0)
    cq = lax.broadcasted_iota(jnp.int32, (BLK, BLK), 1)
    ksum = ksum_ref[0]

    w = []
    for hh in range(2):
        wf = jnp.where((rowd >= HEAD_DIM * hh) & (rowd < HEAD_DIM * (hh + 1)), qt, 0.0)
        w.append(wf.astype(BF16))
        gate = jnp.dot(ksum, wf, precision=HIGHEST, preferred_element_type=F32)
        sel_ref[hh] = _top3_rows(gate, rownf, rown < i)
        acc_ref[hh] = jnp.zeros((LANES, BLK), F32)

    def kblock(n):
        return k_ref[0, pl.ds(pl.multiple_of(n * BLK, BLK), BLK), :]

    def scores(hh, kb, n, kind):
        s = _dot(kb, w[hh])
        if kind == "own":
            return jnp.where(cq >= rk, s + btab_ref[hh, 0], NEG)
        if kind == "prev":
            s = s + btab_ref[hh, 1]
        return jnp.where(sel_ref[hh, pl.ds(n, 1), :] > 0.5, s, NEG)

    def fold8(x, op):
        return op(x.reshape(BLK // 8, 8, BLK), axis=0)

    def pass1(n, kind, ms):
        kb = kblock(n)
        return tuple(jnp.maximum(ms[hh], fold8(scores(hh, kb, n, kind), jnp.max)) for hh in range(2))

    def pass2(n, kind, m, ls):
        kb = kblock(n)
        vtb = vt_ref[0, n]
        out = []
        for hh in range(2):
            p = jnp.exp2(scores(hh, kb, n, kind) - m[hh])
            acc_ref[hh] += _dot(vtb, p.astype(BF16))
            out.append(ls[hh] + fold8(p, jnp.sum))
        return tuple(out)

    n_far = jnp.maximum(i - 1, 0)
    n_prev = jnp.maximum(i - 1, 0)
    ms = (jnp.full((8, BLK), LOWEST, F32),) * 2
    ms = lax.fori_loop(0, n_far, lambda n, c: pass1(n, "far", c), ms)
    ms = pass1(n_prev, "prev", ms)
    ms = pass1(i, "own", ms)
    m = tuple(jnp.max(x, axis=0, keepdims=True) for x in ms)

    ls = (jnp.zeros((8, BLK), F32),) * 2
    ls = lax.fori_loop(0, n_far, lambda n, c: pass2(n, "far", m, c), ls)
    ls = pass2(n_prev, "prev", m, ls)
    ls = pass2(i, "own", m, ls)

    o0 = acc_ref[0] / jnp.sum(ls[0], axis=0, keepdims=True)
    o1 = acc_ref[1] / jnp.sum(ls[1], axis=0, keepdims=True)
    o_ref[0] = jnp.where(rowd < HEAD_DIM, o0, o1).T.astype(BF16)


def _attention(qt, kb, vt, ksum, btab):
    bsz, _, t = qt.shape
    nb = t // BLK
    return pl.pallas_call(
        _attn_kernel,
        grid=(bsz, D_ATT // LANES, nb),
        in_specs=[pl.BlockSpec((1, LANES, BLK), lambda b, hp, i: (b, hp, i)),
                  pl.BlockSpec((1, t, LANES), lambda b, hp, i: (b, 0, hp)),
                  pl.BlockSpec((1, nb, LANES, BLK), lambda b, hp, i: (b, 0, hp, 0)),
                  pl.BlockSpec((1, nb, LANES), lambda b, hp, i: (b, 0, hp)),
                  pl.BlockSpec((2, 2, BLK, BLK), lambda b, hp, i: (hp, 0, 0, 0))],
        out_specs=pl.BlockSpec((1, BLK, LANES), lambda b, hp, i: (b, i, hp)),
        out_shape=jax.ShapeDtypeStruct((bsz, t, D_ATT), BF16),
        scratch_shapes=[pltpu.VMEM((2, nb, BLK), F32), pltpu.VMEM((2, LANES, BLK), F32)],
        compiler_params=_cparams(("arbitrary", "arbitrary", "arbitrary")),
        name="moba_attention",
    )(qt, kb, vt, ksum, btab)


def _outproj_kernel(x_ref, yab_ref, yc_ref, w_ref, g_ref, b_ref, wr_ref, br_ref, x1_ref, comb_ref):
    mix = _dot(yab_ref[...], w_ref[0:512, :]) + _dot(yc_ref[...], w_ref[512:1024, :])
    x1 = _ln(ALPHA * x_ref[...] + mix, g_ref[...], b_ref[...])
    x1_ref[...] = x1
    lg = jnp.dot(x1, wr_ref[...], precision=HIGHEST, preferred_element_type=F32) + br_ref[...]
    lanef = lax.broadcasted_iota(jnp.int32, lg.shape, 1).astype(F32)
    isg = (lanef >= N_EXPERTS) & (lanef < N_EXPERTS + N_GROUPS)
    gl = jnp.where(isg, lg, LOWEST)
    gexp = jnp.where(isg, jnp.exp(gl - jnp.max(gl, axis=1, keepdims=True)), 0.0)
    gprob = gexp / jnp.sum(gexp, axis=1, keepdims=True)
    g_w = jnp.max(gprob, axis=1, keepdims=True)
    g_idx = jnp.min(jnp.where(isg & (gprob == g_w), lanef, 1e9), axis=1, keepdims=True) - N_EXPERTS
    ise = (lanef >= EPG * g_idx) & (lanef < EPG * g_idx + EPG)
    el = jnp.where(ise, lg, LOWEST)
    eexp = jnp.where(ise, jnp.exp(el - jnp.max(el, axis=1, keepdims=True)), 0.0)
    eprob = eexp / jnp.sum(eexp, axis=1, keepdims=True)
    p1 = jnp.max(eprob, axis=1, keepdims=True)
    i1 = jnp.min(jnp.where(ise & (eprob == p1), lanef, 1e9), axis=1, keepdims=True)
    hit1 = lanef == i1
    rest = jnp.where(ise, jnp.where(hit1, -1.0, eprob), -1.0)
    p2 = jnp.max(rest, axis=1, keepdims=True)
    i2 = jnp.min(jnp.where(rest == p2, jnp.where(ise, jnp.where(hit1, 1e9, lanef), 1e9), 1e9),
                 axis=1, keepdims=True)
    hit2 = lanef == i2
    den = p1 + p2
    comb_ref[...] = jnp.where(hit1, g_w * (p1 / den), jnp.where(hit2, g_w * (p2 / den), 0.0))


def _outproj(x, yab, yc, w_out, g, b, wr, br, tm):
    n = x.shape[0]
    full = lambda i: (0, 0)
    return pl.pallas_call(
        _outproj_kernel,
        grid=(n // tm,),
        in_specs=[pl.BlockSpec((tm, D_MODEL), lambda i: (i, 0)),
                  pl.BlockSpec((tm, 512), lambda i: (i, 0)),
                  pl.BlockSpec((tm, 512), lambda i: (i, 0)),
                  pl.BlockSpec((D_MODEL, D_MODEL), full),
                  pl.BlockSpec((1, D_MODEL), full), pl.BlockSpec((1, D_MODEL), full),
                  pl.BlockSpec((D_MODEL, 128), full), pl.BlockSpec((1, 128), full)],
        out_specs=[pl.BlockSpec((tm, D_MODEL), lambda i: (i, 0)),
                   pl.BlockSpec((tm, 128), lambda i: (i, 0))],
        out_shape=[jax.ShapeDtypeStruct((n, D_MODEL), F32), jax.ShapeDtypeStruct((n, 128), F32)],
        compiler_params=_cparams(("arbitrary",)),
        name="outproj_ln_route",
    )(x, yab, yc, w_out, g, b, wr, br)


def _moe_kernel(x_ref, comb_ref, wgu_ref, wd_ref, g_ref, b_ref, o_ref, xb_ref, acc_ref):
    e = pl.program_id(1)

    @pl.when(e == 0)
    def _():
        xb_ref[...] = x_ref[...].astype(BF16)
        acc_ref[...] = jnp.zeros(acc_ref.shape, F32)

    gu = _dot(xb_ref[...], wgu_ref[0])
    hid = _silu(gu[:, 0:D_EXPERT]) * gu[:, D_EXPERT:2 * D_EXPERT]
    comb = comb_ref[...]
    lane = lax.broadcasted_iota(jnp.int32, comb.shape, 1)
    c = jnp.sum(jnp.where(lane == e, comb, 0.0), axis=1, keepdims=True)
    acc_ref[...] += _dot((hid * c).astype(BF16), wd_ref[0])

    @pl.when(e == pl.num_programs(1) - 1)
    def _():
        o_ref[...] = _ln(ALPHA * x_ref[...] + acc_ref[...], g_ref[...], b_ref[...])


def _moe(x1, comb, wgu, wd, g, b, tm):
    n = x1.shape[0]
    return pl.pallas_call(
        _moe_kernel,
        grid=(n // tm, N_EXPERTS),
        in_specs=[pl.BlockSpec((tm, D_MODEL), lambda i, e: (i, 0)),
                  pl.BlockSpec((tm, 128), lambda i, e: (i, 0)),
                  pl.BlockSpec((1, D_MODEL, 2 * D_EXPERT), lambda i, e: (e, 0, 0)),
                  pl.BlockSpec((1, D_EXPERT, D_MODEL), lambda i, e: (e, 0, 0)),
                  pl.BlockSpec((1, D_MODEL), lambda i, e: (0, 0)),
                  pl.BlockSpec((1, D_MODEL), lambda i, e: (0, 0))],
        out_specs=pl.BlockSpec((tm, D_MODEL), lambda i, e: (i, 0)),
        out_shape=jax.ShapeDtypeStruct((n, D_MODEL), F32),
        scratch_shapes=[pltpu.VMEM((tm, D_MODEL), BF16), pltpu.VMEM((tm, D_MODEL), F32)],
        compiler_params=_cparams(("arbitrary", "arbitrary")),
        name="moe_ffn_ln",
    )(x1, comb, wgu, wd, g, b)


def _sinproj_kernel(*refs, apply_ln):
    if apply_ln:
        (x_ref, lng_ref, lnb_ref, w_ref, caw_ref, cab_ref, calg_ref, calb_ref, cbw_ref, sta_ref, stb_ref,
         h_ref, q_ref, k_ref, v_ref, yab_ref, u_ref, ch_ref) = refs
    else:
        (x_ref, w_ref, caw_ref, cab_ref, calg_ref, calb_ref, cbw_ref, sta_ref, stb_ref,
         q_ref, k_ref, v_ref, yab_ref, u_ref, ch_ref) = refs
    x = x_ref[...]
    if apply_ln:
        x = _ln(x, lng_ref[...], lnb_ref[...])
        h_ref[...] = x
    xb = x.astype(BF16)

    def seg(a, b):
        return _dot(xb, w_ref[:, a:b])

    u = seg(0, 256) * jax.nn.sigmoid(seg(256, 512))
    u_ref[...] = u
    acc = caw_ref[CONF_WIDTH - 1:CONF_WIDTH, :] * u
    for j in range(CONF_WIDTH - 1):
        acc = acc + caw_ref[j:j + 1, :] * sta_ref[0, j]
    y_a = _silu(_ln(acc + cab_ref[...], calg_ref[...], calb_ref[...]))
    yab_ref[:, 0:256] = y_a.astype(BF16)

    ch = seg(768, 1024) * seg(1024, 1280)
    ch_ref[...] = ch
    accb = cbw_ref[SCONV_WIDTH - 1:SCONV_WIDTH, :] * ch
    for j in range(SCONV_WIDTH - 1):
        accb = accb + cbw_ref[j:j + 1, :] * stb_ref[0, j]
    yab_ref[:, 256:512] = (seg(512, 768) * accb).astype(BF16)

    q_ref[...] = seg(1280, 1792) * ATT_SCALE
    k_ref[...] = seg(1792, 2304)
    v_ref[...] = seg(2304, 2816)


def _sinproj(x, ln_g, ln_b, w_in, caw, cab, calg, calb, cbw, sta_t, stb_t, layer, apply_ln):
    n = x.shape[0]
    full = lambda i: (0, 0)
    in_specs = [pl.BlockSpec((n, D_MODEL), full)]
    args = [x]
    if apply_ln:
        in_specs += [pl.BlockSpec((1, D_MODEL), full), pl.BlockSpec((1, D_MODEL), full)]
        args += [ln_g, ln_b]
    in_specs += [pl.BlockSpec((D_MODEL, D_IN), full),
                 pl.BlockSpec((CONF_WIDTH, D_CONF), full),
                 pl.BlockSpec((1, D_CONF), full), pl.BlockSpec((1, D_CONF), full), pl.BlockSpec((1, D_CONF), full),
                 pl.BlockSpec((SCONV_WIDTH, D_SCONV), full),
                 pl.BlockSpec((1, CONF_WIDTH - 1, n, D_CONF), lambda i: (layer, 0, 0, 0)),
                 pl.BlockSpec((1, SCONV_WIDTH - 1, n, D_SCONV), lambda i: (layer, 0, 0, 0))]
    args += [w_in, caw, cab, calg, calb, cbw, sta_t, stb_t]
    out_specs, out_shape = [], []
    if apply_ln:
        out_specs.append(pl.BlockSpec((n, D_MODEL), full))
        out_shape.append(jax.ShapeDtypeStruct((n, D_MODEL), F32))
    out_specs += [pl.BlockSpec((n, D_ATT), full)] * 3 + [pl.BlockSpec((n, 512), full),
                                                        pl.BlockSpec((n, D_CONF), full),
                                                        pl.BlockSpec((n, D_SCONV), full)]
    out_shape += [jax.ShapeDtypeStruct((n, D_ATT), F32)] * 3 + [jax.ShapeDtypeStruct((n, 512), BF16),
                                                               jax.ShapeDtypeStruct((n, D_CONF), F32),
                                                               jax.ShapeDtypeStruct((n, D_SCONV), F32)]
    return pl.pallas_call(
        functools.partial(_sinproj_kernel, apply_ln=apply_ln),
        grid=(1,),
        in_specs=in_specs, out_specs=out_specs, out_shape=out_shape,
        compiler_params=_cparams(("arbitrary",)),
        name="sample_inproj_conv",
    )(*args)


N_PAGES = 16
N_PAST_BLK = N_PAGES * PAGE // BLK
SEL_STRIDE = 16


def _page_scores(kt_page, qcol):
    return jnp.sum(kt_page * qcol, axis=0, keepdims=True)


def _sgate_kernel(pt_ref, qt_ref, ptv_ref, *refs):
    krefs = refs[:N_PAGES]
    out_ref = refs[N_PAGES]
    lane1 = lax.broadcasted_iota(jnp.int32, (1, LANES), 1)
    rows = []
    for hd in range(N_HEADS):
        qcol = jnp.broadcast_to(qt_ref[0, :, hd:hd + 1], (HEAD_DIM, PAGE))
        gh = jnp.zeros((1, LANES), F32)
        for n in range(N_PAST_BLK):
            sc = _page_scores(krefs[2 * n][0, 0, hd], qcol) + _page_scores(krefs[2 * n + 1][0, 0, hd], qcol)
            gh = jnp.where(lane1 == n, jnp.sum(sc, axis=1, keepdims=True), gh)
        rows.append(gh)
    g = jnp.concatenate(rows, axis=0)
    lane = lax.broadcasted_iota(jnp.int32, (N_HEADS, LANES), 1)
    lanef = lane.astype(F32)
    g = jnp.where(lane < N_PAST_BLK, g, LOWEST)
    ptf = ptv_ref[0].astype(F32)
    l16 = lax.broadcasted_iota(jnp.int32, (N_HEADS, N_PAGES), 1).astype(F32)
    res = jnp.zeros((N_HEADS, LANES), F32)
    for r in range(TOPK):
        mx = jnp.max(g, axis=1, keepdims=True)
        idx = jnp.min(jnp.where(g == mx, lanef, 1e9), axis=1, keepdims=True)
        g = jnp.where(lanef == idx, LOWEST, g)
        pg0 = jnp.sum(jnp.where(l16 == 2.0 * idx, ptf, 0.0), axis=1, keepdims=True)
        pg1 = jnp.sum(jnp.where(l16 == 2.0 * idx + 1.0, ptf, 0.0), axis=1, keepdims=True)
        res = jnp.where(lane == 2 * r, pg0, res)
        res = jnp.where(lane == 2 * r + 1, pg1, res)
        res = jnp.where(lane == 8 + r, idx, res)
    out_ref[0] = res.astype(jnp.int32)


def _sgate(page_table, qt8, cache_kt, layer):
    nseq = page_table.shape[0]
    ptv = page_table.reshape(nseq, 1, N_PAGES)

    def kspec(j):
        return pl.BlockSpec((1, 1, N_HEADS, HEAD_DIM, PAGE), lambda b, pt: (layer, pt[b * N_PAGES + j], 0, 0, 0))

    grid_spec = pltpu.PrefetchScalarGridSpec(
        num_scalar_prefetch=1,
        grid=(nseq,),
        in_specs=[pl.BlockSpec((1, HEAD_DIM, N_HEADS), lambda b, pt: (b, 0, 0)),
                  pl.BlockSpec((1, 1, N_PAGES), lambda b, pt: (b, 0, 0))]
                 + [kspec(j) for j in range(N_PAGES)],
        out_specs=pl.BlockSpec((1, N_HEADS, LANES), lambda b, pt: (b, 0, 0)),
    )
    return pl.pallas_call(
        _sgate_kernel,
        grid_spec=grid_spec,
        out_shape=jax.ShapeDtypeStruct((nseq, N_HEADS, LANES), jnp.int32),
        compiler_params=_cparams(("arbitrary",)),
        name="sample_block_gate",
    )(page_table.reshape(-1), qt8, ptv, *([cache_kt] * N_PAGES))


N_SEL_PAGES = TOPK * BLK // PAGE


def _sattn_kernel(sel_ref, rb_ref, q_ref, kn_ref, qt_ref, vnt_ref, stab_ref, *refs):
    np_ = N_HEADS * N_SEL_PAGES
    krefs = refs[:np_]
    vrefs = refs[np_:2 * np_]
    o_ref = refs[2 * np_]
    b = pl.program_id(0)
    s_self_all = jnp.sum(q_ref[0] * kn_ref[0], axis=1, keepdims=True)
    for hd in range(N_HEADS):
        qcol = jnp.broadcast_to(qt_ref[0, :, hd:hd + 1], (HEAD_DIM, PAGE))
        base = (b * N_HEADS + hd) * SEL_STRIDE
        c31 = rb_ref[N_BUCKETS - 1, hd]
        scores = []
        for s in range(N_SEL_PAGES):
            sc = _page_scores(krefs[hd * N_SEL_PAGES + s][0, 0, 0], qcol)
            blk = sel_ref[base + 8 + s // 2]
            tab = stab_ref[hd, (s % 2):(s % 2) + 1, :]
            scores.append(sc + jnp.where(blk == N_PAST_BLK - 1, tab, c31))
        s_self = s_self_all[hd:hd + 1, :] + rb_ref[0, hd]
        m = s_self
        for sc in scores:
            m = jnp.maximum(m, jnp.max(sc, axis=1, keepdims=True))
        e_self = jnp.exp(s_self - m)
        den = e_self
        acc = jnp.zeros((HEAD_DIM, PAGE), F32)
        for s in range(N_SEL_PAGES):
            e = jnp.exp(scores[s] - m)
            den = den + jnp.sum(e, axis=1, keepdims=True)
            acc = acc + vrefs[hd * N_SEL_PAGES + s][0, 0, 0] * e
        num = jnp.sum(acc, axis=1, keepdims=True) + e_self * vnt_ref[0, :, hd:hd + 1]
        o_ref[0, :, hd:hd + 1] = num / den


def _sattn(sel, rel_bias, q8, kn8, qt8, vnt8, stab, cache_kt, cache_vt, layer):
    nseq = q8.shape[0]

    def pspec(hd, s):
        return pl.BlockSpec((1, 1, 1, HEAD_DIM, PAGE),
                            lambda b, sl: (layer, sl[(b * N_HEADS + hd) * SEL_STRIDE + s], hd, 0, 0))

    def page_specs():
        return [pspec(hd, s) for hd in range(N_HEADS) for s in range(N_SEL_PAGES)]

    def row():
        return pl.BlockSpec((1, N_HEADS, HEAD_DIM), lambda b, sl: (b, 0, 0))

    def col():
        return pl.BlockSpec((1, HEAD_DIM, N_HEADS), lambda b, sl: (b, 0, 0))

    grid_spec = pltpu.PrefetchScalarGridSpec(
        num_scalar_prefetch=1,
        grid=(nseq,),
        in_specs=[pl.BlockSpec(memory_space=pltpu.SMEM), row(), row(), col(), col(),
                  pl.BlockSpec((N_HEADS, 8, PAGE), lambda b, sl: (0, 0, 0))] + page_specs() + page_specs(),
        out_specs=col(),
    )
    npg = N_HEADS * N_SEL_PAGES
    return pl.pallas_call(
        _sattn_kernel,
        grid_spec=grid_spec,
        out_shape=jax.ShapeDtypeStruct((nseq, HEAD_DIM, N_HEADS), F32),
        compiler_params=_cparams(("arbitrary",)),
        name="sample_attention",
    )(sel.reshape(-1), rel_bias, q8, kn8, qt8, vnt8, stab, *([cache_kt] * npg), *([cache_vt] * npg))


TM_OUT = 512
TM_MOE = 1024


def kernel(x_prompt, x_sample, cache_k, cache_v, state_conv_a, state_conv_b, page_table, ln_in_g, ln_in_b, w_in, conv_a_w, conv_a_b, conv_a_ln_g, conv_a_ln_b, conv_b_w, rel_bias, w_out, ln1_g, ln1_b, w_group, b_group, w_router, b_router, w_gate, w_up, w_down, ln2_g, ln2_b):
    bsz, t, _ = x_prompt.shape
    nseq = x_sample.shape[0]
    n_p = bsz * t
    row = lambda a: a.reshape(1, -1)

    btab, stab = _bias_tables(rel_bias)
    sta_t = state_conv_a.transpose(0, 2, 1, 3)
    stb_t = state_conv_b.transpose(0, 2, 1, 3)
    cache_kt = cache_k.transpose(0, 1, 2, 4, 3)
    cache_vt = cache_v.transpose(0, 1, 2, 4, 3)
    x_s = x_sample.reshape(nseq, D_MODEL)
    kpages = jnp.zeros((DEPTH, bsz, t // PAGE, N_HEADS, HEAD_DIM, PAGE), F32)
    vpages = jnp.zeros((DEPTH, bsz, t // PAGE, N_HEADS, HEAD_DIM, PAGE), F32)

    h_p = x_prompt
    h_s = x_s
    ap_l, bp_l, ks_l, vs_l, us_l, cs_l = [], [], [], [], [], []
    for l in range(DEPTH):
        first = l == 0
        w_in_b = w_in[l].astype(BF16)
        w_out_b = w_out[l].astype(BF16)
        wgu = jnp.concatenate([w_gate[l], w_up[l]], axis=-1).astype(BF16)
        wd = w_down[l].astype(BF16)
        wr = jnp.pad(jnp.concatenate([w_router[l], w_group[l]], axis=-1), ((0, 0), (0, 128 - N_EXPERTS - N_GROUPS)))
        br = jnp.pad(jnp.concatenate([b_router[l], b_group[l]]), (0, 128 - N_EXPERTS - N_GROUPS)).reshape(1, 128)
        conv_args = (conv_a_w[l], row(conv_a_b[l]), row(conv_a_ln_g[l]), row(conv_a_ln_b[l]), conv_b_w[l])

        outs = _inproj(h_p, row(ln_in_g), row(ln_in_b), w_in_b, *conv_args, kpages, vpages, l, apply_ln=first)
        if first:
            h_p, outs = outs[0], outs[1:]
        qt, kb, vt, kpages, vpages, yab, ksum, new_a, new_b = outs
        yc = _attention(qt, kb, vt, ksum.reshape(bsz, t // BLK, D_ATT), btab)
        x1, comb = _outproj(h_p.reshape(n_p, D_MODEL), yab.reshape(n_p, 512), yc.reshape(n_p, D_ATT),
                            w_out_b, row(ln1_g[l]), row(ln1_b[l]), wr, br, TM_OUT)
        h_p = _moe(x1, comb, wgu, wd, row(ln2_g[l]), row(ln2_b[l]), TM_MOE).reshape(bsz, t, D_MODEL)
        ap_l.append(new_a)
        bp_l.append(new_b)

        souts = _sinproj(h_s, row(ln_in_g), row(ln_in_b), w_in_b, *conv_args, sta_t, stb_t, l, apply_ln=first)
        if first:
            h_s, souts = souts[0], souts[1:]
        q_s, k_s, v_s, yab_s, u_s, ch_s = souts
        q8 = q_s.reshape(nseq, N_HEADS, HEAD_DIM)
        k8 = k_s.reshape(nseq, N_HEADS, HEAD_DIM)
        v8 = v_s.reshape(nseq, N_HEADS, HEAD_DIM)
        qt8 = q8.transpose(0, 2, 1)
        sel = _sgate(page_table, qt8, cache_kt, l)[:, :, :SEL_STRIDE]
        yc_s = _sattn(sel, rel_bias, q8, k8, qt8, v8.transpose(0, 2, 1), stab, cache_kt, cache_vt, l)
        yc_s = yc_s.transpose(0, 2, 1).reshape(nseq, D_ATT).astype(BF16)
        x1_s, comb_s = _outproj(h_s, yab_s, yc_s, w_out_b, row(ln1_g[l]), row(ln1_b[l]), wr, br, nseq)
        h_s = _moe(x1_s, comb_s, wgu, wd, row(ln2_g[l]), row(ln2_b[l]), nseq)
        ks_l.append(k8.reshape(nseq, N_HEADS, 1, HEAD_DIM))
        vs_l.append(v8.reshape(nseq, N_HEADS, 1, HEAD_DIM))
        us_l.append(u_s)
        cs_l.append(ch_s)

    new_a_s = jnp.concatenate([state_conv_a[:, :, 1:], jnp.stack(us_l)[:, :, None, :]], axis=2)
    new_b_s = jnp.concatenate([state_conv_b[:, :, 1:], jnp.stack(cs_l)[:, :, None, :]], axis=2)
    return (h_p, h_s.reshape(nseq, 1, D_MODEL),
            kpages.transpose(0, 1, 2, 3, 5, 4), vpages.transpose(0, 1, 2, 3, 5, 4),
            jnp.stack(ap_l), jnp.stack(bp_l), jnp.stack(ks_l), jnp.stack(vs_l), new_a_s, new_b_s)
```

```python
import functools
import math

import numpy as np
import jax
import jax.numpy as jnp
from jax import lax
from jax.experimental import pallas as pl
from jax.experimental.pallas import tpu as pltpu

D_MODEL = 1024
HEAD_DIM = 64
D_CONF = 256
D_SCONV = 256
D_ATT = 512
N_HEADS = 8
CONF_WIDTH = 31
SCONV_WIDTH = 3
PAGE = 128
BLK = 256
TOPK = 3
N_BUCKETS = 32
MAX_DISTANCE = 128
N_GROUPS = 4
EPG = 4
N_EXPERTS = 16
D_EXPERT = 256
DEPTH = 4
ALPHA = (2 * DEPTH) ** 0.25
LN_EPS = 1e-5
NEG = -1e30
LOWEST = -3e38
ATT_SCALE = HEAD_DIM ** -0.5
LOG2E = math.log2(math.e)
D_IN = 2 * D_CONF + 3 * D_SCONV + 3 * D_ATT
LANES = 128

F32 = jnp.float32
BF16 = jnp.bfloat16
HIGHEST = lax.Precision.HIGHEST

VMEM_LIMIT = 56 * 1024 * 1024


def _cparams(sem):
    return pltpu.CompilerParams(dimension_semantics=sem, vmem_limit_bytes=VMEM_LIMIT)


def _ln(x, g, b):
    mu = jnp.mean(x, axis=-1, keepdims=True)
    xc = x - mu
    var = jnp.mean(xc * xc, axis=-1, keepdims=True)
    return xc * lax.rsqrt(var + LN_EPS) * g + b


def _silu(x):
    return x * jax.nn.sigmoid(x)


def _dot(a, b):
    return jnp.dot(a, b, preferred_element_type=F32)


def _top3_rows(gate, rowf, valid):
    g = jnp.where(valid, gate, NEG)
    sel = jnp.zeros(gate.shape, F32)
    for _ in range(TOPK):
        mx = jnp.max(g, axis=0, keepdims=True)
        idx = jnp.min(jnp.where(g == mx, rowf, 1e9), axis=0, keepdims=True)
        hit = rowf == idx
        sel = jnp.where(hit, 1.0, sel)
        g = jnp.where(hit, LOWEST, g)
    return jnp.where(valid, sel, 0.0)


def _t5_bucket_np(dist):
    n = np.maximum(dist, 0)
    max_exact = N_BUCKETS // 2
    nf = np.maximum(n, 1).astype(np.float32)
    large = max_exact + (np.log(nf / np.float32(max_exact)) / np.float32(math.log(MAX_DISTANCE / max_exact))
                         * np.float32(N_BUCKETS - max_exact)).astype(np.int32)
    large = np.minimum(large, N_BUCKETS - 1)
    return np.where(n < max_exact, n, large).astype(np.int32)


def _bias_kernel(rb_ref, idx0_ref, idx1_ref, idxs_ref, btab_ref, stab_ref):
    h = pl.program_id(0)
    idx0 = idx0_ref[...]
    idx1 = idx1_ref[...]
    idxs = idxs_ref[...]
    t0 = jnp.zeros(idx0.shape, F32)
    t1 = jnp.zeros(idx1.shape, F32)
    ts = jnp.zeros(idxs.shape, F32)
    for bkt in range(N_BUCKETS):
        val = rb_ref[bkt, h]
        t0 = jnp.where(idx0 == bkt, val, t0)
        t1 = jnp.where(idx1 == bkt, val, t1)
        ts = jnp.where(idxs == bkt, val, ts)
    far = rb_ref[N_BUCKETS - 1, h]
    btab_ref[0, 0] = (t0 - far) * LOG2E
    btab_ref[0, 1] = (t1 - far) * LOG2E
    stab_ref[0] = ts


def _bias_tables(rel_bias):
    r = np.arange(BLK)[:, None]
    c = np.arange(BLK)[None, :]
    idx0 = _t5_bucket_np(r - c)
    idx1 = _t5_bucket_np(BLK + r - c)
    idxs = np.zeros((8, PAGE), np.int32)
    idxs[0:2] = _t5_bucket_np(BLK - np.arange(BLK)).reshape(2, PAGE)
    return pl.pallas_call(
        _bias_kernel,
        grid=(N_HEADS,),
        in_specs=[pl.BlockSpec(memory_space=pltpu.SMEM),
                  pl.BlockSpec((BLK, BLK), lambda h: (0, 0)),
                  pl.BlockSpec((BLK, BLK), lambda h: (0, 0)),
                  pl.BlockSpec((8, PAGE), lambda h: (0, 0))],
        out_specs=[pl.BlockSpec((1, 2, BLK, BLK), lambda h: (h, 0, 0, 0)),
                   pl.BlockSpec((1, 8, PAGE), lambda h: (h, 0, 0))],
        out_shape=[jax.ShapeDtypeStruct((N_HEADS, 2, BLK, BLK), F32),
                   jax.ShapeDtypeStruct((N_HEADS, 8, PAGE), F32)],
        compiler_params=_cparams(("arbitrary",)),
        name="bias_tables",
    )(rel_bias, jnp.asarray(idx0), jnp.asarray(idx1), jnp.asarray(idxs))


TM1 = BLK


def _inproj_kernel(*refs, apply_ln):
    if apply_ln:
        (x_ref, lng_ref, lnb_ref, w_ref, caw_ref, cab_ref, calg_ref, calb_ref, cbw_ref, _kin, _vin,
         h_ref, q_ref, qt_ref, kt_ref, vb_ref, kpg_ref, vpg_ref, yab_ref, ksum_ref, newa_ref, newb_ref,
         ua_ref, ub_ref) = refs
    else:
        (x_ref, w_ref, caw_ref, cab_ref, calg_ref, calb_ref, cbw_ref, _kin, _vin,
         q_ref, qt_ref, kt_ref, vb_ref, kpg_ref, vpg_ref, yab_ref, ksum_ref, newa_ref, newb_ref,
         ua_ref, ub_ref) = refs
    ti = pl.program_id(1)
    last = pl.num_programs(1) - 1
    x = x_ref[0]
    if apply_ln:
        x = _ln(x, lng_ref[...], lnb_ref[...])
        h_ref[0] = x
    xb = x.astype(BF16)

    def seg(a, b):
        return _dot(xb, w_ref[:, a:b])

    @pl.when(ti == 0)
    def _():
        ua_ref[0:32, :] = jnp.zeros((32, D_CONF), F32)
        ub_ref[0:8, :] = jnp.zeros((8, D_SCONV), F32)

    u = seg(0, 256) * jax.nn.sigmoid(seg(256, 512))
    ua_ref[32:32 + TM1, :] = u
    acc = jnp.zeros((TM1, D_CONF), F32)
    for j in range(CONF_WIDTH):
        acc = acc + caw_ref[j:j + 1, :] * ua_ref[2 + j:2 + j + TM1, :]
    y_a = _silu(_ln(acc + cab_ref[...], calg_ref[...], calb_ref[...]))
    yab_ref[0, :, 0:256] = y_a.astype(BF16)

    ch = seg(768, 1024) * seg(1024, 1280)
    ub_ref[8:8 + TM1, :] = ch
    accb = jnp.zeros((TM1, D_SCONV), F32)
    for j in range(SCONV_WIDTH):
        accb = accb + cbw_ref[j:j + 1, :] * ub_ref[6 + j:6 + j + TM1, :]
    yab_ref[0, :, 256:512] = (seg(512, 768) * accb).astype(BF16)

    @pl.when(ti == last)
    def _():
        newa_ref[0] = ua_ref[TM1 + 2:TM1 + 32, :]
        newb_ref[0] = ub_ref[TM1 + 6:TM1 + 8, :]

    ua_ref[0:32, :] = ua_ref[TM1:TM1 + 32, :]
    ub_ref[0:8, :] = ub_ref[TM1:TM1 + 8, :]

    q = seg(1280, 1792) * (ATT_SCALE * LOG2E)
    q_ref[0] = q
    qt_ref[0] = q.T
    k = seg(1792, 2304)
    v = seg(2304, 2816)
    kt = k.T
    vt = v.T
    kt_ref[0, 0] = kt.astype(BF16)
    vb_ref[0] = v.astype(BF16)
    ksum_ref[0, 0] = jnp.sum(k, axis=0, keepdims=True)
    for p in range(TM1 // PAGE):
        for hd in range(N_HEADS):
            kpg_ref[0, 0, p, hd] = kt[hd * HEAD_DIM:(hd + 1) * HEAD_DIM, p * PAGE:(p + 1) * PAGE]
            vpg_ref[0, 0, p, hd] = vt[hd * HEAD_DIM:(hd + 1) * HEAD_DIM, p * PAGE:(p + 1) * PAGE]


def _inproj(x, ln_g, ln_b, w_in, caw, cab, calg, calb, cbw, kpages, vpages, layer, apply_ln):
    bsz, t, _ = x.shape
    nt = t // TM1
    full2 = lambda b, i: (0, 0)
    in_specs = [pl.BlockSpec((1, TM1, D_MODEL), lambda b, i: (b, i, 0))]
    args = [x]
    if apply_ln:
        in_specs += [pl.BlockSpec((1, D_MODEL), full2), pl.BlockSpec((1, D_MODEL), full2)]
        args += [ln_g, ln_b]
    in_specs += [pl.BlockSpec((D_MODEL, D_IN), full2),
                 pl.BlockSpec((CONF_WIDTH, D_CONF), full2),
                 pl.BlockSpec((1, D_CONF), full2), pl.BlockSpec((1, D_CONF), full2),
                 pl.BlockSpec((1, D_CONF), full2),
                 pl.BlockSpec((SCONV_WIDTH, D_SCONV), full2),
                 pl.BlockSpec(memory_space=pl.ANY), pl.BlockSpec(memory_space=pl.ANY)]
    args += [w_in, caw, cab, calg, calb, cbw, kpages, vpages]
    kpages_in = len(args) - 2
    n_extra = 1 if apply_ln else 0
    page_block = (1, 1, TM1 // PAGE, N_HEADS, HEAD_DIM, PAGE)
    page_map = lambda b, i: (layer, b, i, 0, 0, 0)
    out_specs, out_shape = [], []
    if apply_ln:
        out_specs.append(pl.BlockSpec((1, TM1, D_MODEL), lambda b, i: (b, i, 0)))
        out_shape.append(jax.ShapeDtypeStruct((bsz, t, D_MODEL), F32))
    out_specs += [
        pl.BlockSpec((1, TM1, D_ATT), lambda b, i: (b, i, 0)),
        pl.BlockSpec((1, D_ATT, TM1), lambda b, i: (b, 0, i)),
        pl.BlockSpec((1, 1, D_ATT, TM1), lambda b, i: (b, i, 0, 0)),
        pl.BlockSpec((1, TM1, D_ATT), lambda b, i: (b, i, 0)),
        pl.BlockSpec(page_block, page_map),
        pl.BlockSpec(page_block, page_map),
        pl.BlockSpec((1, TM1, 512), lambda b, i: (b, i, 0)),
        pl.BlockSpec((1, 1, 1, D_ATT), lambda b, i: (b, i, 0, 0)),
        pl.BlockSpec((1, CONF_WIDTH - 1, D_CONF), lambda b, i: (b, 0, 0)),
        pl.BlockSpec((1, SCONV_WIDTH - 1, D_SCONV), lambda b, i: (b, 0, 0)),
    ]
    out_shape += [
        jax.ShapeDtypeStruct((bsz, t, D_ATT), F32),
        jax.ShapeDtypeStruct((bsz, D_ATT, t), F32),
        jax.ShapeDtypeStruct((bsz, nt, D_ATT, TM1), BF16),
        jax.ShapeDtypeStruct((bsz, t, D_ATT), BF16),
        jax.ShapeDtypeStruct(kpages.shape, F32),
        jax.ShapeDtypeStruct(vpages.shape, F32),
        jax.ShapeDtypeStruct((bsz, t, 512), BF16),
        jax.ShapeDtypeStruct((bsz, nt, 1, D_ATT), F32),
        jax.ShapeDtypeStruct((bsz, CONF_WIDTH - 1, D_CONF), F32),
        jax.ShapeDtypeStruct((bsz, SCONV_WIDTH - 1, D_SCONV), F32),
    ]
    return pl.pallas_call(
        functools.partial(_inproj_kernel, apply_ln=apply_ln),
        grid=(bsz, nt),
        in_specs=in_specs, out_specs=out_specs, out_shape=out_shape,
        input_output_aliases={kpages_in: n_extra + 4, kpages_in + 1: n_extra + 5},
        scratch_shapes=[pltpu.VMEM((32 + TM1, D_CONF), F32), pltpu.VMEM((8 + TM1, D_SCONV), F32)],
        compiler_params=_cparams(("arbitrary", "arbitrary")),
        name="inproj_conv",
    )(*args)


PEN_LANES = 16


def _attn_kernel(q_ref, qt_ref, kt_ref, v_ref, ksum_ref, btab_ref, o_ref, m_ref, l_ref, acc_ref, s_ref):
    i = pl.program_id(2)
    nb = ksum_ref.shape[1]
    qf = q_ref[0]
    qt = qt_ref[0]
    lane = lax.broadcasted_iota(jnp.int32, (BLK, LANES), 1)
    rowd = lax.broadcasted_iota(jnp.int32, (LANES, BLK), 0)
    rown = lax.broadcasted_iota(jnp.int32, (nb, BLK), 0)
    rownf = rown.astype(F32)
    rq = lax.broadcasted_iota(jnp.int32, (BLK, BLK), 0)
    ck = lax.broadcasted_iota(jnp.int32, (BLK, BLK), 1)
    ksum = ksum_ref[0]
    shift_r = lax.broadcasted_iota(jnp.int32, (nb, LANES), 0)
    shift_c = lax.broadcasted_iota(jnp.int32, (nb, LANES), 1)

    qa, head_rows, pen_base = [], [], []
    for hh in range(2):
        base = HEAD_DIM * (1 - hh)
        in_t = (rowd >= HEAD_DIM * hh) & (rowd < HEAD_DIM * (hh + 1))
        wf = jnp.where(in_t, qt, 0.0)
        gate = jnp.dot(ksum, wf, precision=HIGHEST, preferred_element_type=F32)
        sel_t = _top3_rows(gate, rownf, rown < i)
        pen = jnp.where(sel_t > 0.5, 0.0, NEG).T
        place = jnp.where(shift_c == shift_r + base, 1.0, 0.0).astype(BF16)
        pen_wide = _dot(pen.astype(BF16), place)
        in_q = (lane >= HEAD_DIM * hh) & (lane < HEAD_DIM * (hh + 1))
        qa.append(jnp.where(in_q, qf, pen_wide).astype(BF16))
        head_rows.append(jnp.where(in_t, 1.0, 0.0).astype(BF16))
        pen_base.append(base)
        m_ref[hh] = jnp.full((BLK, LANES), LOWEST, F32)
        l_ref[hh] = jnp.zeros((BLK, LANES), F32)
        acc_ref[hh] = jnp.zeros((BLK, LANES), F32)

    def scores(hh, ktn, n, kind):
        wk = ktn * head_rows[hh]
        if kind == "own":
            return jnp.where(ck <= rq, _dot(qa[hh], wk) + btab_ref[hh, 0], NEG)
        wk = wk + jnp.where(rowd == pen_base[hh] + n, 1.0, 0.0).astype(BF16)
        s = _dot(qa[hh], wk)
        if kind == "prev":
            s = s + btab_ref[hh, 1]
        return s

    def pass1(blocks):
        kts = [kt_ref[0, n] for n, _, _ in blocks]
        for hh in range(2):
            mx = m_ref[hh]
            for (n, kind, slot), ktn in zip(blocks, kts):
                s = scores(hh, ktn, n, kind)
                s_ref[hh, slot] = s
                mx = jnp.maximum(mx, jnp.maximum(s[:, :LANES], s[:, LANES:]))
            m_ref[hh] = mx

    def pass2(blocks):
        vs = [v_ref[0, pl.ds(pl.multiple_of(n * BLK, BLK), BLK), :] for n, _ in blocks]
        for hh in range(2):
            mb = m_ref[hh]
            lsum = l_ref[hh]
            osum = acc_ref[hh]
            for (n, slot), vn in zip(blocks, vs):
                p0 = jnp.exp2(s_ref[hh, slot, :, :LANES] - mb)
                p1 = jnp.exp2(s_ref[hh, slot, :, LANES:] - mb)
                lsum = lsum + (p0 + p1)
                osum = osum + _dot(jnp.concatenate([p0, p1], axis=1).astype(BF16), vn)
            l_ref[hh] = lsum
            acc_ref[hh] = osum

    n_far = jnp.maximum(i - 1, 0)
    n_prev = jnp.maximum(i - 1, 0)
    own_slot = nb
    n_pairs = n_far // 2
    odd = n_far - 2 * n_pairs

    def far1(j, c):
        pass1([(2 * j, "far", 2 * j), (2 * j + 1, "far", 2 * j + 1)])
        return c

    def far2(j, c):
        pass2([(2 * j, 2 * j), (2 * j + 1, 2 * j + 1)])
        return c

    lax.fori_loop(0, n_pairs, far1, 0)

    @pl.when(odd == 1)
    def _():
        pass1([(n_far - 1, "far", n_far - 1)])

    pass1([(n_prev, "prev", n_prev), (i, "own", own_slot)])
    for hh in range(2):
        m_ref[hh] = jnp.broadcast_to(jnp.max(m_ref[hh], axis=1, keepdims=True), (BLK, LANES))
    lax.fori_loop(0, n_pairs, far2, 0)

    @pl.when(odd == 1)
    def _():
        pass2([(n_far - 1, n_far - 1)])

    pass2([(n_prev, n_prev), (i, own_slot)])

    o0 = acc_ref[0] / jnp.sum(l_ref[0], axis=1, keepdims=True)
    o1 = acc_ref[1] / jnp.sum(l_ref[1], axis=1, keepdims=True)
    o_ref[0] = jnp.where(lane < HEAD_DIM, o0, o1).astype(BF16)


def _attention(q, qt, kt, vb, ksum, btab):
    bsz, t, _ = q.shape
    nb = t // BLK
    assert nb <= PEN_LANES
    stat = pltpu.VMEM((2, BLK, LANES), F32)
    return pl.pallas_call(
        _attn_kernel,
        grid=(bsz, D_ATT // LANES, nb),
        in_specs=[pl.BlockSpec((1, BLK, LANES), lambda b, hp, i: (b, i, hp)),
                  pl.BlockSpec((1, LANES, BLK), lambda b, hp, i: (b, hp, i)),
                  pl.BlockSpec((1, nb, LANES, BLK), lambda b, hp, i: (b, 0, hp, 0)),
                  pl.BlockSpec((1, t, LANES), lambda b, hp, i: (b, 0, hp)),
                  pl.BlockSpec((1, nb, LANES), lambda b, hp, i: (b, 0, hp)),
                  pl.BlockSpec((2, 2, BLK, BLK), lambda b, hp, i: (hp, 0, 0, 0))],
        out_specs=pl.BlockSpec((1, BLK, LANES), lambda b, hp, i: (b, i, hp)),
        out_shape=jax.ShapeDtypeStruct((bsz, t, D_ATT), BF16),
        scratch_shapes=[stat, stat, stat, pltpu.VMEM((2, nb + 1, BLK, BLK), F32)],
        compiler_params=_cparams(("arbitrary", "arbitrary", "arbitrary")),
        name="moba_attention",
    )(q, qt, kt, vb, ksum, btab)


def _outproj_kernel(x_ref, yab_ref, yc_ref, w_ref, g_ref, b_ref, wr_ref, br_ref, x1_ref, comb_ref):
    mix = _dot(yab_ref[...], w_ref[0:512, :]) + _dot(yc_ref[...], w_ref[512:1024, :])
    x1 = _ln(ALPHA * x_ref[...] + mix, g_ref[...], b_ref[...])
    x1_ref[...] = x1
    lg = jnp.dot(x1, wr_ref[...], precision=HIGHEST, preferred_element_type=F32) + br_ref[...]
    lanef = lax.broadcasted_iota(jnp.int32, lg.shape, 1).astype(F32)
    isg = (lanef >= N_EXPERTS) & (lanef < N_EXPERTS + N_GROUPS)
    gl = jnp.where(isg, lg, LOWEST)
    gexp = jnp.where(isg, jnp.exp(gl - jnp.max(gl, axis=1, keepdims=True)), 0.0)
    gprob = gexp / jnp.sum(gexp, axis=1, keepdims=True)
    g_w = jnp.max(gprob, axis=1, keepdims=True)
    g_idx = jnp.min(jnp.where(isg & (gprob == g_w), lanef, 1e9), axis=1, keepdims=True) - N_EXPERTS
    ise = (lanef >= EPG * g_idx) & (lanef < EPG * g_idx + EPG)
    el = jnp.where(ise, lg, LOWEST)
    eexp = jnp.where(ise, jnp.exp(el - jnp.max(el, axis=1, keepdims=True)), 0.0)
    eprob = eexp / jnp.sum(eexp, axis=1, keepdims=True)
    p1 = jnp.max(eprob, axis=1, keepdims=True)
    i1 = jnp.min(jnp.where(ise & (eprob == p1), lanef, 1e9), axis=1, keepdims=True)
    hit1 = lanef == i1
    rest = jnp.where(ise, jnp.where(hit1, -1.0, eprob), -1.0)
    p2 = jnp.max(rest, axis=1, keepdims=True)
    i2 = jnp.min(jnp.where(rest == p2, jnp.where(ise, jnp.where(hit1, 1e9, lanef), 1e9), 1e9),
                 axis=1, keepdims=True)
    hit2 = lanef == i2
    den = p1 + p2
    comb_ref[...] = jnp.where(hit1, g_w * (p1 / den), jnp.where(hit2, g_w * (p2 / den), 0.0))


def _outproj(x, yab, yc, w_out, g, b, wr, br, tm):
    n = x.shape[0]
    full = lambda i: (0, 0)
    return pl.pallas_call(
        _outproj_kernel,
        grid=(n // tm,),
        in_specs=[pl.BlockSpec((tm, D_MODEL), lambda i: (i, 0)),
                  pl.BlockSpec((tm, 512), lambda i: (i, 0)),
                  pl.BlockSpec((tm, 512), lambda i: (i, 0)),
                  pl.BlockSpec((D_MODEL, D_MODEL), full),
                  pl.BlockSpec((1, D_MODEL), full), pl.BlockSpec((1, D_MODEL), full),
                  pl.BlockSpec((D_MODEL, 128), full), pl.BlockSpec((1, 128), full)],
        out_specs=[pl.BlockSpec((tm, D_MODEL), lambda i: (i, 0)),
                   pl.BlockSpec((tm, 128), lambda i: (i, 0))],
        out_shape=[jax.ShapeDtypeStruct((n, D_MODEL), F32), jax.ShapeDtypeStruct((n, 128), F32)],
        compiler_params=_cparams(("arbitrary",)),
        name="outproj_ln_route",
    )(x, yab, yc, w_out, g, b, wr, br)


def _moe_kernel(x_ref, comb_ref, wgu_ref, wd_ref, g_ref, b_ref, o_ref, xb_ref, acc_ref):
    e = pl.program_id(1)

    @pl.when(e == 0)
    def _():
        xb_ref[...] = x_ref[...].astype(BF16)
        acc_ref[...] = jnp.zeros(acc_ref.shape, F32)

    gu = _dot(xb_ref[...], wgu_ref[0])
    hid = _silu(gu[:, 0:D_EXPERT]) * gu[:, D_EXPERT:2 * D_EXPERT]
    comb = comb_ref[...]
    lane = lax.broadcasted_iota(jnp.int32, comb.shape, 1)
    c = jnp.sum(jnp.where(lane == e, comb, 0.0), axis=1, keepdims=True)
    acc_ref[...] += _dot((hid * c).astype(BF16), wd_ref[0])

    @pl.when(e == pl.num_programs(1) - 1)
    def _():
        o_ref[...] = _ln(ALPHA * x_ref[...] + acc_ref[...], g_ref[...], b_ref[...])


def _moe(x1, comb, wgu, wd, g, b, tm):
    n = x1.shape[0]
    return pl.pallas_call(
        _moe_kernel,
        grid=(n // tm, N_EXPERTS),
        in_specs=[pl.BlockSpec((tm, D_MODEL), lambda i, e: (i, 0)),
                  pl.BlockSpec((tm, 128), lambda i, e: (i, 0)),
                  pl.BlockSpec((1, D_MODEL, 2 * D_EXPERT), lambda i, e: (e, 0, 0)),
                  pl.BlockSpec((1, D_EXPERT, D_MODEL), lambda i, e: (e, 0, 0)),
                  pl.BlockSpec((1, D_MODEL), lambda i, e: (0, 0)),
                  pl.BlockSpec((1, D_MODEL), lambda i, e: (0, 0))],
        out_specs=pl.BlockSpec((tm, D_MODEL), lambda i, e: (i, 0)),
        out_shape=jax.ShapeDtypeStruct((n, D_MODEL), F32),
        scratch_shapes=[pltpu.VMEM((tm, D_MODEL), BF16), pltpu.VMEM((tm, D_MODEL), F32)],
        compiler_params=_cparams(("arbitrary", "arbitrary")),
        name="moe_ffn_ln",
    )(x1, comb, wgu, wd, g, b)


def _sinproj_kernel(*refs, apply_ln):
    if apply_ln:
        (x_ref, lng_ref, lnb_ref, w_ref, caw_ref, cab_ref, calg_ref, calb_ref, cbw_ref, sta_ref, stb_ref,
         h_ref, q_ref, k_ref, v_ref, yab_ref, u_ref, ch_ref) = refs
    else:
        (x_ref, w_ref, caw_ref, cab_ref, calg_ref, calb_ref, cbw_ref, sta_ref, stb_ref,
         q_ref, k_ref, v_ref, yab_ref, u_ref, ch_ref) = refs
    x = x_ref[...]
    if apply_ln:
        x = _ln(x, lng_ref[...], lnb_ref[...])
        h_ref[...] = x
    xb = x.astype(BF16)

    def seg(a, b):
        return _dot(xb, w_ref[:, a:b])

    u = seg(0, 256) * jax.nn.sigmoid(seg(256, 512))
    u_ref[...] = u
    acc = caw_ref[CONF_WIDTH - 1:CONF_WIDTH, :] * u
    for j in range(CONF_WIDTH - 1):
        acc = acc + caw_ref[j:j + 1, :] * sta_ref[0, j]
    y_a = _silu(_ln(acc + cab_ref[...], calg_ref[...], calb_ref[...]))
    yab_ref[:, 0:256] = y_a.astype(BF16)

    ch = seg(768, 1024) * seg(1024, 1280)
    ch_ref[...] = ch
    accb = cbw_ref[SCONV_WIDTH - 1:SCONV_WIDTH, :] * ch
    for j in range(SCONV_WIDTH - 1):
        accb = accb + cbw_ref[j:j + 1, :] * stb_ref[0, j]
    yab_ref[:, 256:512] = (seg(512, 768) * accb).astype(BF16)

    q_ref[...] = seg(1280, 1792) * ATT_SCALE
    k_ref[...] = seg(1792, 2304)
    v_ref[...] = seg(2304, 2816)


def _sinproj(x, ln_g, ln_b, w_in, caw, cab, calg, calb, cbw, sta_t, stb_t, layer, apply_ln):
    n = x.shape[0]
    full = lambda i: (0, 0)
    in_specs = [pl.BlockSpec((n, D_MODEL), full)]
    args = [x]
    if apply_ln:
        in_specs += [pl.BlockSpec((1, D_MODEL), full), pl.BlockSpec((1, D_MODEL), full)]
        args += [ln_g, ln_b]
    in_specs += [pl.BlockSpec((D_MODEL, D_IN), full),
                 pl.BlockSpec((CONF_WIDTH, D_CONF), full),
                 pl.BlockSpec((1, D_CONF), full), pl.BlockSpec((1, D_CONF), full), pl.BlockSpec((1, D_CONF), full),
                 pl.BlockSpec((SCONV_WIDTH, D_SCONV), full),
                 pl.BlockSpec((1, CONF_WIDTH - 1, n, D_CONF), lambda i: (layer, 0, 0, 0)),
                 pl.BlockSpec((1, SCONV_WIDTH - 1, n, D_SCONV), lambda i: (layer, 0, 0, 0))]
    args += [w_in, caw, cab, calg, calb, cbw, sta_t, stb_t]
    out_specs, out_shape = [], []
    if apply_ln:
        out_specs.append(pl.BlockSpec((n, D_MODEL), full))
        out_shape.append(jax.ShapeDtypeStruct((n, D_MODEL), F32))
    out_specs += [pl.BlockSpec((n, D_ATT), full)] * 3 + [pl.BlockSpec((n, 512), full),
                                                        pl.BlockSpec((n, D_CONF), full),
                                                        pl.BlockSpec((n, D_SCONV), full)]
    out_shape += [jax.ShapeDtypeStruct((n, D_ATT), F32)] * 3 + [jax.ShapeDtypeStruct((n, 512), BF16),
                                                               jax.ShapeDtypeStruct((n, D_CONF), F32),
                                                               jax.ShapeDtypeStruct((n, D_SCONV), F32)]
    return pl.pallas_call(
        functools.partial(_sinproj_kernel, apply_ln=apply_ln),
        grid=(1,),
        in_specs=in_specs, out_specs=out_specs, out_shape=out_shape,
        compiler_params=_cparams(("arbitrary",)),
        name="sample_inproj_conv",
    )(*args)


N_PAGES = 16
N_PAST_BLK = N_PAGES * PAGE // BLK
SEL_STRIDE = 16


def _page_scores(kt_page, qcol):
    return jnp.sum(kt_page * qcol, axis=0, keepdims=True)


def _sgate_kernel(pt_ref, qt_ref, ptv_ref, *refs):
    krefs = refs[:N_PAGES]
    out_ref = refs[N_PAGES]
    lane1 = lax.broadcasted_iota(jnp.int32, (1, LANES), 1)
    rows = []
    for hd in range(N_HEADS):
        qcol = jnp.broadcast_to(qt_ref[0, :, hd:hd + 1], (HEAD_DIM, PAGE))
        gh = jnp.zeros((1, LANES), F32)
        for n in range(N_PAST_BLK):
            sc = _page_scores(krefs[2 * n][0, 0, hd], qcol) + _page_scores(krefs[2 * n + 1][0, 0, hd], qcol)
            gh = jnp.where(lane1 == n, jnp.sum(sc, axis=1, keepdims=True), gh)
        rows.append(gh)
    g = jnp.concatenate(rows, axis=0)
    lane = lax.broadcasted_iota(jnp.int32, (N_HEADS, LANES), 1)
    lanef = lane.astype(F32)
    g = jnp.where(lane < N_PAST_BLK, g, LOWEST)
    ptf = ptv_ref[0].astype(F32)
    l16 = lax.broadcasted_iota(jnp.int32, (N_HEADS, N_PAGES), 1).astype(F32)
    res = jnp.zeros((N_HEADS, LANES), F32)
    for r in range(TOPK):
        mx = jnp.max(g, axis=1, keepdims=True)
        idx = jnp.min(jnp.where(g == mx, lanef, 1e9), axis=1, keepdims=True)
        g = jnp.where(lanef == idx, LOWEST, g)
        pg0 = jnp.sum(jnp.where(l16 == 2.0 * idx, ptf, 0.0), axis=1, keepdims=True)
        pg1 = jnp.sum(jnp.where(l16 == 2.0 * idx + 1.0, ptf, 0.0), axis=1, keepdims=True)
        res = jnp.where(lane == 2 * r, pg0, res)
        res = jnp.where(lane == 2 * r + 1, pg1, res)
        res = jnp.where(lane == 8 + r, idx, res)
    out_ref[0] = res.astype(jnp.int32)


def _sgate(page_table, qt8, cache_kt, layer):
    nseq = page_table.shape[0]
    ptv = page_table.reshape(nseq, 1, N_PAGES)

    def kspec(j):
        return pl.BlockSpec((1, 1, N_HEADS, HEAD_DIM, PAGE), lambda b, pt: (layer, pt[b * N_PAGES + j], 0, 0, 0))

    grid_spec = pltpu.PrefetchScalarGridSpec(
        num_scalar_prefetch=1,
        grid=(nseq,),
        in_specs=[pl.BlockSpec((1, HEAD_DIM, N_HEADS), lambda b, pt: (b, 0, 0)),
                  pl.BlockSpec((1, 1, N_PAGES), lambda b, pt: (b, 0, 0))]
                 + [kspec(j) for j in range(N_PAGES)],
        out_specs=pl.BlockSpec((1, N_HEADS, LANES), lambda b, pt: (b, 0, 0)),
    )
    return pl.pallas_call(
        _sgate_kernel,
        grid_spec=grid_spec,
        out_shape=jax.ShapeDtypeStruct((nseq, N_HEADS, LANES), jnp.int32),
        compiler_params=_cparams(("arbitrary",)),
        name="sample_block_gate",
    )(page_table.reshape(-1), qt8, ptv, *([cache_kt] * N_PAGES))


N_SEL_PAGES = TOPK * BLK // PAGE


def _sattn_kernel(sel_ref, rb_ref, q_ref, kn_ref, qt_ref, vnt_ref, stab_ref, *refs):
    np_ = N_HEADS * N_SEL_PAGES
    krefs = refs[:np_]
    vrefs = refs[np_:2 * np_]
    o_ref = refs[2 * np_]
    b = pl.program_id(0)
    s_self_all = jnp.sum(q_ref[0] * kn_ref[0], axis=1, keepdims=True)
    for hd in range(N_HEADS):
        qcol = jnp.broadcast_to(qt_ref[0, :, hd:hd + 1], (HEAD_DIM, PAGE))
        base = (b * N_HEADS + hd) * SEL_STRIDE
        c31 = rb_ref[N_BUCKETS - 1, hd]
        scores = []
        for s in range(N_SEL_PAGES):
            sc = _page_scores(krefs[hd * N_SEL_PAGES + s][0, 0, 0], qcol)
            blk = sel_ref[base + 8 + s // 2]
            tab = stab_ref[hd, (s % 2):(s % 2) + 1, :]
            scores.append(sc + jnp.where(blk == N_PAST_BLK - 1, tab, c31))
        s_self = s_self_all[hd:hd + 1, :] + rb_ref[0, hd]
        m = s_self
        for sc in scores:
            m = jnp.maximum(m, jnp.max(sc, axis=1, keepdims=True))
        e_self = jnp.exp(s_self - m)
        den = e_self
        acc = jnp.zeros((HEAD_DIM, PAGE), F32)
        for s in range(N_SEL_PAGES):
            e = jnp.exp(scores[s] - m)
            den = den + jnp.sum(e, axis=1, keepdims=True)
            acc = acc + vrefs[hd * N_SEL_PAGES + s][0, 0, 0] * e
        num = jnp.sum(acc, axis=1, keepdims=True) + e_self * vnt_ref[0, :, hd:hd + 1]
        o_ref[0, :, hd:hd + 1] = num / den


def _sattn(sel, rel_bias, q8, kn8, qt8, vnt8, stab, cache_kt, cache_vt, layer):
    nseq = q8.shape[0]

    def pspec(hd, s):
        return pl.BlockSpec((1, 1, 1, HEAD_DIM, PAGE),
                            lambda b, sl: (layer, sl[(b * N_HEADS + hd) * SEL_STRIDE + s], hd, 0, 0))

    def page_specs():
        return [pspec(hd, s) for hd in range(N_HEADS) for s in range(N_SEL_PAGES)]

    def row():
        return pl.BlockSpec((1, N_HEADS, HEAD_DIM), lambda b, sl: (b, 0, 0))

    def col():
        return pl.BlockSpec((1, HEAD_DIM, N_HEADS), lambda b, sl: (b, 0, 0))

    grid_spec = pltpu.PrefetchScalarGridSpec(
        num_scalar_prefetch=1,
        grid=(nseq,),
        in_specs=[pl.BlockSpec(memory_space=pltpu.SMEM), row(), row(), col(), col(),
                  pl.BlockSpec((N_HEADS, 8, PAGE), lambda b, sl: (0, 0, 0))] + page_specs() + page_specs(),
        out_specs=col(),
    )
    npg = N_HEADS * N_SEL_PAGES
    return pl.pallas_call(
        _sattn_kernel,
        grid_spec=grid_spec,
        out_shape=jax.ShapeDtypeStruct((nseq, HEAD_DIM, N_HEADS), F32),
        compiler_params=_cparams(("arbitrary",)),
        name="sample_attention",
    )(sel.reshape(-1), rel_bias, q8, kn8, qt8, vnt8, stab, *([cache_kt] * npg), *([cache_vt] * npg))


TM_OUT = 512
TM_MOE = 1024


def kernel(x_prompt, x_sample, cache_k, cache_v, state_conv_a, state_conv_b, page_table, ln_in_g, ln_in_b, w_in, conv_a_w, conv_a_b, conv_a_ln_g, conv_a_ln_b, conv_b_w, rel_bias, w_out, ln1_g, ln1_b, w_group, b_group, w_router, b_router, w_gate, w_up, w_down, ln2_g, ln2_b):
    bsz, t, _ = x_prompt.shape
    nseq = x_sample.shape[0]
    n_p = bsz * t
    row = lambda a: a.reshape(1, -1)

    btab, stab = _bias_tables(rel_bias)
    sta_t = state_conv_a.transpose(0, 2, 1, 3)
    stb_t = state_conv_b.transpose(0, 2, 1, 3)
    cache_kt = cache_k.transpose(0, 1, 2, 4, 3)
    cache_vt = cache_v.transpose(0, 1, 2, 4, 3)
    x_s = x_sample.reshape(nseq, D_MODEL)
    kpages = jnp.zeros((DEPTH, bsz, t // PAGE, N_HEADS, HEAD_DIM, PAGE), F32)
    vpages = jnp.zeros((DEPTH, bsz, t // PAGE, N_HEADS, HEAD_DIM, PAGE), F32)

    h_p = x_prompt
    h_s = x_s
    ap_l, bp_l, ks_l, vs_l, us_l, cs_l = [], [], [], [], [], []
    for l in range(DEPTH):
        first = l == 0
        w_in_b = w_in[l].astype(BF16)
        w_out_b = w_out[l].astype(BF16)
        wgu = jnp.concatenate([w_gate[l], w_up[l]], axis=-1).astype(BF16)
        wd = w_down[l].astype(BF16)
        wr = jnp.pad(jnp.concatenate([w_router[l], w_group[l]], axis=-1), ((0, 0), (0, 128 - N_EXPERTS - N_GROUPS)))
        br = jnp.pad(jnp.concatenate([b_router[l], b_group[l]]), (0, 128 - N_EXPERTS - N_GROUPS)).reshape(1, 128)
        conv_args = (conv_a_w[l], row(conv_a_b[l]), row(conv_a_ln_g[l]), row(conv_a_ln_b[l]), conv_b_w[l])

        outs = _inproj(h_p, row(ln_in_g), row(ln_in_b), w_in_b, *conv_args, kpages, vpages, l, apply_ln=first)
        if first:
            h_p, outs = outs[0], outs[1:]
        q, qt, kt, vb, kpages, vpages, yab, ksum, new_a, new_b = outs
        yc = _attention(q, qt, kt, vb, ksum.reshape(bsz, t // BLK, D_ATT), btab)
        x1, comb = _outproj(h_p.reshape(n_p, D_MODEL), yab.reshape(n_p, 512), yc.reshape(n_p, D_ATT),
                            w_out_b, row(ln1_g[l]), row(ln1_b[l]), wr, br, TM_OUT)
        h_p = _moe(x1, comb, wgu, wd, row(ln2_g[l]), row(ln2_b[l]), TM_MOE).reshape(bsz, t, D_MODEL)
        ap_l.append(new_a)
        bp_l.append(new_b)

        souts = _sinproj(h_s, row(ln_in_g), row(ln_in_b), w_in_b, *conv_args, sta_t, stb_t, l, apply_ln=first)
        if first:
            h_s, souts = souts[0], souts[1:]
        q_s, k_s, v_s, yab_s, u_s, ch_s = souts
        q8 = q_s.reshape(nseq, N_HEADS, HEAD_DIM)
        k8 = k_s.reshape(nseq, N_HEADS, HEAD_DIM)
        v8 = v_s.reshape(nseq, N_HEADS, HEAD_DIM)
        qt8 = q8.transpose(0, 2, 1)
        sel = _sgate(page_table, qt8, cache_kt, l)[:, :, :SEL_STRIDE]
        yc_s = _sattn(sel, rel_bias, q8, k8, qt8, v8.transpose(0, 2, 1), stab, cache_kt, cache_vt, l)
        yc_s = yc_s.transpose(0, 2, 1).reshape(nseq, D_ATT).astype(BF16)
        x1_s, comb_s = _outproj(h_s, yab_s, yc_s, w_out_b, row(ln1_g[l]), row(ln1_b[l]), wr, br, nseq)
        h_s = _moe(x1_s, comb_s, wgu, wd, row(ln2_g[l]), row(ln2_b[l]), nseq)
        ks_l.append(k8.reshape(nseq, N_HEADS, 1, HEAD_DIM))
        vs_l.append(v8.reshape(nseq, N_HEADS, 1, HEAD_DIM))
        us_l.append(u_s)
        cs_l.append(ch_s)

    new_a_s = jnp.concatenate([state_conv_a[:, :, 1:], jnp.stack(us_l)[:, :, None, :]], axis=2)
    new_b_s = jnp.concatenate([state_conv_b[:, :, 1:], jnp.stack(cs_l)[:, :, None, :]], axis=2)
    return (h_p, h_s.reshape(nseq, 1, D_MODEL),
            kpages.transpose(0, 1, 2, 3, 5, 4), vpages.transpose(0, 1, 2, 3, 5, 4),
            jnp.stack(ap_l), jnp.stack(bp_l), jnp.stack(ks_l), jnp.stack(vs_l), new_a_s, new_b_s)
```

```python
import functools
import math

import numpy as np
import jax
import jax.numpy as jnp
from jax import lax
from jax.experimental import pallas as pl
from jax.experimental.pallas import tpu as pltpu

D_MODEL = 1024
HEAD_DIM = 64
D_CONF = 256
D_SCONV = 256
D_ATT = 512
N_HEADS = 8
CONF_WIDTH = 31
SCONV_WIDTH = 3
PAGE = 128
BLK = 256
TOPK = 3
N_BUCKETS = 32
MAX_DISTANCE = 128
N_GROUPS = 4
EPG = 4
N_EXPERTS = 16
D_EXPERT = 256
DEPTH = 4
ALPHA = (2 * DEPTH) ** 0.25
LN_EPS = 1e-5
NEG = -1e30
LOWEST = -3e38
ATT_SCALE = HEAD_DIM ** -0.5
LOG2E = math.log2(math.e)
D_IN = 2 * D_CONF + 3 * D_SCONV + 3 * D_ATT
LANES = 128
PEN_LANES = HEAD_DIM

F32 = jnp.float32
BF16 = jnp.bfloat16
HIGHEST = lax.Precision.HIGHEST

VMEM_LIMIT = 56 * 1024 * 1024


def _cparams(sem):
    return pltpu.CompilerParams(dimension_semantics=sem, vmem_limit_bytes=VMEM_LIMIT)


def _ln(x, g, b):
    mu = jnp.mean(x, axis=-1, keepdims=True)
    xc = x - mu
    var = jnp.mean(xc * xc, axis=-1, keepdims=True)
    return xc * lax.rsqrt(var + LN_EPS) * g + b


def _silu(x):
    return x * jax.nn.sigmoid(x)


def _dot(a, b):
    return jnp.dot(a, b, preferred_element_type=F32)


def _top3_rows(gate, rowf, valid):
    g = jnp.where(valid, gate, NEG)
    sel = jnp.zeros(gate.shape, F32)
    for _ in range(TOPK):
        mx = jnp.max(g, axis=0, keepdims=True)
        idx = jnp.min(jnp.where(g == mx, rowf, 1e9), axis=0, keepdims=True)
        hit = rowf == idx
        sel = jnp.where(hit, 1.0, sel)
        g = jnp.where(hit, LOWEST, g)
    return jnp.where(valid, sel, 0.0)


def _t5_bucket_np(dist):
    n = np.maximum(dist, 0)
    max_exact = N_BUCKETS // 2
    nf = np.maximum(n, 1).astype(np.float32)
    large = max_exact + (np.log(nf / np.float32(max_exact)) / np.float32(math.log(MAX_DISTANCE / max_exact))
                         * np.float32(N_BUCKETS - max_exact)).astype(np.int32)
    large = np.minimum(large, N_BUCKETS - 1)
    return np.where(n < max_exact, n, large).astype(np.int32)


def _bias_kernel(rb_ref, idx0_ref, idx1_ref, idxs_ref, btab_ref, stab_ref):
    h = pl.program_id(0)
    idx0 = idx0_ref[...]
    idx1 = idx1_ref[...]
    idxs = idxs_ref[...]
    t0 = jnp.zeros(idx0.shape, F32)
    t1 = jnp.zeros(idx1.shape, F32)
    ts = jnp.zeros(idxs.shape, F32)
    for bkt in range(N_BUCKETS):
        val = rb_ref[bkt, h]
        t0 = jnp.where(idx0 == bkt, val, t0)
        t1 = jnp.where(idx1 == bkt, val, t1)
        ts = jnp.where(idxs == bkt, val, ts)
    far = rb_ref[N_BUCKETS - 1, h]
    btab_ref[0, 0] = (t0 - far) * LOG2E
    btab_ref[0, 1] = (t1 - far) * LOG2E
    stab_ref[0] = ts


def _bias_tables(rel_bias):
    r = np.arange(BLK)[:, None]
    c = np.arange(BLK)[None, :]
    idx0 = _t5_bucket_np(r - c)
    idx1 = _t5_bucket_np(BLK + r - c)
    idxs = np.zeros((8, PAGE), np.int32)
    idxs[0:2] = _t5_bucket_np(BLK - np.arange(BLK)).reshape(2, PAGE)
    return pl.pallas_call(
        _bias_kernel,
        grid=(N_HEADS,),
        in_specs=[pl.BlockSpec(memory_space=pltpu.SMEM),
                  pl.BlockSpec((BLK, BLK), lambda h: (0, 0)),
                  pl.BlockSpec((BLK, BLK), lambda h: (0, 0)),
                  pl.BlockSpec((8, PAGE), lambda h: (0, 0))],
        out_specs=[pl.BlockSpec((1, 2, BLK, BLK), lambda h: (h, 0, 0, 0)),
                   pl.BlockSpec((1, 8, PAGE), lambda h: (h, 0, 0))],
        out_shape=[jax.ShapeDtypeStruct((N_HEADS, 2, BLK, BLK), F32),
                   jax.ShapeDtypeStruct((N_HEADS, 8, PAGE), F32)],
        compiler_params=_cparams(("arbitrary",)),
        name="bias_tables",
    )(rel_bias, jnp.asarray(idx0), jnp.asarray(idx1), jnp.asarray(idxs))


TM1 = BLK


def _inproj_kernel(*refs, apply_ln):
    if apply_ln:
        (x_ref, lng_ref, lnb_ref, w_ref, caw_ref, cab_ref, calg_ref, calb_ref, cbw_ref, _kin, _vin,
         h_ref, qa_ref, kt_ref, vb_ref, kpg_ref, vpg_ref, yab_ref, newa_ref, newb_ref,
         ua_ref, ub_ref, ksh_ref) = refs
    else:
        (x_ref, w_ref, caw_ref, cab_ref, calg_ref, calb_ref, cbw_ref, _kin, _vin,
         qa_ref, kt_ref, vb_ref, kpg_ref, vpg_ref, yab_ref, newa_ref, newb_ref,
         ua_ref, ub_ref, ksh_ref) = refs
    ti = pl.program_id(1)
    last = pl.num_programs(1) - 1
    x = x_ref[0]
    if apply_ln:
        x = _ln(x, lng_ref[...], lnb_ref[...])
        h_ref[0] = x
    xb = x.astype(BF16)

    def seg(a, b):
        return _dot(xb, w_ref[:, a:b])

    @pl.when(ti == 0)
    def _():
        ua_ref[0:32, :] = jnp.zeros((32, D_CONF), F32)
        ub_ref[0:8, :] = jnp.zeros((8, D_SCONV), F32)
        ksh_ref[...] = jnp.zeros(ksh_ref.shape, F32)

    u = seg(0, 256) * jax.nn.sigmoid(seg(256, 512))
    ua_ref[32:32 + TM1, :] = u
    acc = jnp.zeros((TM1, D_CONF), F32)
    for j in range(CONF_WIDTH):
        acc = acc + caw_ref[j:j + 1, :] * ua_ref[2 + j:2 + j + TM1, :]
    y_a = _silu(_ln(acc + cab_ref[...], calg_ref[...], calb_ref[...]))
    yab_ref[0, :, 0:256] = y_a.astype(BF16)

    ch = seg(768, 1024) * seg(1024, 1280)
    ub_ref[8:8 + TM1, :] = ch
    accb = jnp.zeros((TM1, D_SCONV), F32)
    for j in range(SCONV_WIDTH):
        accb = accb + cbw_ref[j:j + 1, :] * ub_ref[6 + j:6 + j + TM1, :]
    yab_ref[0, :, 256:512] = (seg(512, 768) * accb).astype(BF16)

    @pl.when(ti == last)
    def _():
        newa_ref[0] = ua_ref[TM1 + 2:TM1 + 32, :]
        newb_ref[0] = ub_ref[TM1 + 6:TM1 + 8, :]

    ua_ref[0:32, :] = ua_ref[TM1:TM1 + 32, :]
    ub_ref[0:8, :] = ub_ref[TM1:TM1 + 8, :]

    q = seg(1280, 1792) * (ATT_SCALE * LOG2E)
    k = seg(1792, 2304)
    v = seg(2304, 2816)

    nb = ksh_ref.shape[0]
    ksh = ksh_ref[...]
    lane = lax.broadcasted_iota(jnp.int32, (TM1, LANES), 1)
    rown = lax.broadcasted_iota(jnp.int32, (nb, TM1), 0)
    rownf = rown.astype(F32)
    shift_r = lax.broadcasted_iota(jnp.int32, (nb, LANES), 0)
    shift_c = lax.broadcasted_iota(jnp.int32, (nb, LANES), 1)
    for hd in range(N_HEADS):
        hh, pair = hd % 2, hd // 2
        base = HEAD_DIM * (1 - hh)
        in_q = (lane >= HEAD_DIM * hh) & (lane < HEAD_DIM * (hh + 1))
        qp = q[:, pair * LANES:(pair + 1) * LANES]
        gate = lax.dot_general(ksh[:, pair * LANES:(pair + 1) * LANES], jnp.where(in_q, qp, 0.0),
                               (((1,), (1,)), ((), ())), precision=HIGHEST,
                               preferred_element_type=F32)
        sel_t = _top3_rows(gate, rownf, rown < ti)
        pen = jnp.where(sel_t > 0.5, 0.0, NEG).T
        place = jnp.where(shift_c == shift_r + base, 1.0, 0.0).astype(BF16)
        pen_wide = _dot(pen.astype(BF16), place)
        qa_ref[0, hd] = jnp.where(in_q, qp, pen_wide).astype(BF16)
    ksh_ref[pl.ds(ti, 1), :] = jnp.sum(k, axis=0, keepdims=True)

    kt = k.T
    vt = v.T
    kt_ref[0, 0] = kt.astype(BF16)
    vb_ref[0] = v.astype(BF16)
    for p in range(TM1 // PAGE):
        for hd in range(N_HEADS):
            kpg_ref[0, 0, p, hd] = kt[hd * HEAD_DIM:(hd + 1) * HEAD_DIM, p * PAGE:(p + 1) * PAGE]
            vpg_ref[0, 0, p, hd] = vt[hd * HEAD_DIM:(hd + 1) * HEAD_DIM, p * PAGE:(p + 1) * PAGE]


def _inproj(x, ln_g, ln_b, w_in, caw, cab, calg, calb, cbw, kpages, vpages, layer, apply_ln):
    bsz, t, _ = x.shape
    nt = t // TM1
    full2 = lambda b, i: (0, 0)
    in_specs = [pl.BlockSpec((1, TM1, D_MODEL), lambda b, i: (b, i, 0))]
    args = [x]
    if apply_ln:
        in_specs += [pl.BlockSpec((1, D_MODEL), full2), pl.BlockSpec((1, D_MODEL), full2)]
        args += [ln_g, ln_b]
    in_specs += [pl.BlockSpec((D_MODEL, D_IN), full2),
                 pl.BlockSpec((CONF_WIDTH, D_CONF), full2),
                 pl.BlockSpec((1, D_CONF), full2), pl.BlockSpec((1, D_CONF), full2),
                 pl.BlockSpec((1, D_CONF), full2),
                 pl.BlockSpec((SCONV_WIDTH, D_SCONV), full2),
                 pl.BlockSpec(memory_space=pl.ANY), pl.BlockSpec(memory_space=pl.ANY)]
    args += [w_in, caw, cab, calg, calb, cbw, kpages, vpages]
    kpages_in = len(args) - 2
    n_extra = 1 if apply_ln else 0
    page_block = (1, 1, TM1 // PAGE, N_HEADS, HEAD_DIM, PAGE)
    page_map = lambda b, i: (layer, b, i, 0, 0, 0)
    out_specs, out_shape = [], []
    if apply_ln:
        out_specs.append(pl.BlockSpec((1, TM1, D_MODEL), lambda b, i: (b, i, 0)))
        out_shape.append(jax.ShapeDtypeStruct((bsz, t, D_MODEL), F32))
    out_specs += [
        pl.BlockSpec((1, N_HEADS, TM1, LANES), lambda b, i: (b, 0, i, 0)),
        pl.BlockSpec((1, 1, D_ATT, TM1), lambda b, i: (b, i, 0, 0)),
        pl.BlockSpec((1, TM1, D_ATT), lambda b, i: (b, i, 0)),
        pl.BlockSpec(page_block, page_map),
        pl.BlockSpec(page_block, page_map),
        pl.BlockSpec((1, TM1, 512), lambda b, i: (b, i, 0)),
        pl.BlockSpec((1, CONF_WIDTH - 1, D_CONF), lambda b, i: (b, 0, 0)),
        pl.BlockSpec((1, SCONV_WIDTH - 1, D_SCONV), lambda b, i: (b, 0, 0)),
    ]
    out_shape += [
        jax.ShapeDtypeStruct((bsz, N_HEADS, t, LANES), BF16),
        jax.ShapeDtypeStruct((bsz, nt, D_ATT, TM1), BF16),
        jax.ShapeDtypeStruct((bsz, t, D_ATT), BF16),
        jax.ShapeDtypeStruct(kpages.shape, F32),
        jax.ShapeDtypeStruct(vpages.shape, F32),
        jax.ShapeDtypeStruct((bsz, t, 512), BF16),
        jax.ShapeDtypeStruct((bsz, CONF_WIDTH - 1, D_CONF), F32),
        jax.ShapeDtypeStruct((bsz, SCONV_WIDTH - 1, D_SCONV), F32),
    ]
    assert nt <= PEN_LANES
    return pl.pallas_call(
        functools.partial(_inproj_kernel, apply_ln=apply_ln),
        grid=(bsz, nt),
        in_specs=in_specs, out_specs=out_specs, out_shape=out_shape,
        input_output_aliases={kpages_in: n_extra + 3, kpages_in + 1: n_extra + 4},
        scratch_shapes=[pltpu.VMEM((32 + TM1, D_CONF), F32), pltpu.VMEM((8 + TM1, D_SCONV), F32),
                        pltpu.VMEM((nt, D_ATT), F32)],
        compiler_params=_cparams(("arbitrary", "arbitrary")),
        name="inproj_conv",
    )(*args)


def _attn_kernel(qa_ref, kt_ref, v_ref, btab_ref, o_ref, m_ref, l_ref, acc_ref, s_ref):
    i = pl.program_id(2)
    nb = kt_ref.shape[1]
    lane = lax.broadcasted_iota(jnp.int32, (BLK, LANES), 1)
    rowd = lax.broadcasted_iota(jnp.int32, (LANES, BLK), 0)
    rq = lax.broadcasted_iota(jnp.int32, (BLK, BLK), 0)
    ck = lax.broadcasted_iota(jnp.int32, (BLK, BLK), 1)

    qa, head_rows, pen_base = [], [], []
    for hh in range(2):
        in_t = (rowd >= HEAD_DIM * hh) & (rowd < HEAD_DIM * (hh + 1))
        qa.append(qa_ref[0, hh])
        head_rows.append(jnp.where(in_t, 1.0, 0.0).astype(BF16))
        pen_base.append(HEAD_DIM * (1 - hh))
        m_ref[hh] = jnp.full((BLK, LANES), LOWEST, F32)
        l_ref[hh] = jnp.zeros((BLK, LANES), F32)
        acc_ref[hh] = jnp.zeros((BLK, LANES), F32)

    def scores(hh, ktn, n, kind):
        wk = ktn * head_rows[hh]
        if kind == "own":
            return jnp.where(ck <= rq, _dot(qa[hh], wk) + btab_ref[hh, 0], NEG)
        wk = wk + jnp.where(rowd == pen_base[hh] + n, 1.0, 0.0).astype(BF16)
        s = _dot(qa[hh], wk)
        if kind == "prev":
            s = s + btab_ref[hh, 1]
        return s

    def pass1(blocks):
        kts = [kt_ref[0, n] for n, _, _ in blocks]
        for hh in range(2):
            mx = m_ref[hh]
            for (n, kind, slot), ktn in zip(blocks, kts):
                s = scores(hh, ktn, n, kind)
                s_ref[hh, slot] = s
                mx = jnp.maximum(mx, jnp.maximum(s[:, :LANES], s[:, LANES:]))
            m_ref[hh] = mx

    def pass2(blocks):
        vs = [v_ref[0, pl.ds(pl.multiple_of(n * BLK, BLK), BLK), :] for n, _ in blocks]
        for hh in range(2):
            mb = m_ref[hh]
            lsum = l_ref[hh]
            osum = acc_ref[hh]
            for (n, slot), vn in zip(blocks, vs):
                p0 = jnp.exp2(s_ref[hh, slot, :, :LANES] - mb)
                p1 = jnp.exp2(s_ref[hh, slot, :, LANES:] - mb)
                lsum = lsum + (p0 + p1)
                osum = osum + _dot(jnp.concatenate([p0, p1], axis=1).astype(BF16), vn)
            l_ref[hh] = lsum
            acc_ref[hh] = osum

    n_far = jnp.maximum(i - 1, 0)
    n_prev = jnp.maximum(i - 1, 0)
    own_slot = nb
    n_pairs = n_far // 2
    odd = n_far - 2 * n_pairs

    def far1(j, c):
        pass1([(2 * j, "far", 2 * j), (2 * j + 1, "far", 2 * j + 1)])
        return c

    def far2(j, c):
        pass2([(2 * j, 2 * j), (2 * j + 1, 2 * j + 1)])
        return c

    lax.fori_loop(0, n_pairs, far1, 0)

    @pl.when(odd == 1)
    def _():
        pass1([(n_far - 1, "far", n_far - 1)])

    pass1([(n_prev, "prev", n_prev), (i, "own", own_slot)])
    for hh in range(2):
        m_ref[hh] = jnp.broadcast_to(jnp.max(m_ref[hh], axis=1, keepdims=True), (BLK, LANES))
    lax.fori_loop(0, n_pairs, far2, 0)

    @pl.when(odd == 1)
    def _():
        pass2([(n_far - 1, n_far - 1)])

    pass2([(n_prev, n_prev), (i, own_slot)])

    o0 = acc_ref[0] / jnp.sum(l_ref[0], axis=1, keepdims=True)
    o1 = acc_ref[1] / jnp.sum(l_ref[1], axis=1, keepdims=True)
    o_ref[0] = jnp.where(lane < HEAD_DIM, o0, o1).astype(BF16)


def _attention(qa, kt, vb, btab):
    bsz, t, _ = vb.shape
    nb = t // BLK
    stat = pltpu.VMEM((2, BLK, LANES), F32)
    return pl.pallas_call(
        _attn_kernel,
        grid=(bsz, D_ATT // LANES, nb),
        in_specs=[pl.BlockSpec((1, 2, BLK, LANES), lambda b, hp, i: (b, hp, i, 0)),
                  pl.BlockSpec((1, nb, LANES, BLK), lambda b, hp, i: (b, 0, hp, 0)),
                  pl.BlockSpec((1, t, LANES), lambda b, hp, i: (b, 0, hp)),
                  pl.BlockSpec((2, 2, BLK, BLK), lambda b, hp, i: (hp, 0, 0, 0))],
        out_specs=pl.BlockSpec((1, BLK, LANES), lambda b, hp, i: (b, i, hp)),
        out_shape=jax.ShapeDtypeStruct((bsz, t, D_ATT), BF16),
        scratch_shapes=[stat, stat, stat, pltpu.VMEM((2, nb + 1, BLK, BLK), F32)],
        compiler_params=_cparams(("arbitrary", "arbitrary", "arbitrary")),
        name="moba_attention",
    )(qa, kt, vb, btab)


def _outproj_kernel(x_ref, yab_ref, yc_ref, w_ref, g_ref, b_ref, wrh_ref, wrl_ref, br_ref, x1_ref, comb_ref):
    mix = _dot(yab_ref[...], w_ref[0:512, :]) + _dot(yc_ref[...], w_ref[512:1024, :])
    x1 = _ln(ALPHA * x_ref[...] + mix, g_ref[...], b_ref[...])
    x1_ref[...] = x1
    x_hi = x1.astype(BF16)
    x_lo = (x1 - x_hi.astype(F32)).astype(BF16)
    lg = (_dot(x_hi, wrh_ref[...]) + (_dot(x_lo, wrh_ref[...]) + _dot(x_hi, wrl_ref[...]))) + br_ref[...]
    lanef = lax.broadcasted_iota(jnp.int32, lg.shape, 1).astype(F32)
    isg = (lanef >= N_EXPERTS) & (lanef < N_EXPERTS + N_GROUPS)
    gl = jnp.where(isg, lg, LOWEST)
    gexp = jnp.where(isg, jnp.exp(gl - jnp.max(gl, axis=1, keepdims=True)), 0.0)
    gprob = gexp / jnp.sum(gexp, axis=1, keepdims=True)
    g_w = jnp.max(gprob, axis=1, keepdims=True)
    g_idx = jnp.min(jnp.where(isg & (gprob == g_w), lanef, 1e9), axis=1, keepdims=True) - N_EXPERTS
    ise = (lanef >= EPG * g_idx) & (lanef < EPG * g_idx + EPG)
    el = jnp.where(ise, lg, LOWEST)
    eexp = jnp.where(ise, jnp.exp(el - jnp.max(el, axis=1, keepdims=True)), 0.0)
    eprob = eexp / jnp.sum(eexp, axis=1, keepdims=True)
    p1 = jnp.max(eprob, axis=1, keepdims=True)
    i1 = jnp.min(jnp.where(ise & (eprob == p1), lanef, 1e9), axis=1, keepdims=True)
    hit1 = lanef == i1
    rest = jnp.where(ise, jnp.where(hit1, -1.0, eprob), -1.0)
    p2 = jnp.max(rest, axis=1, keepdims=True)
    i2 = jnp.min(jnp.where(rest == p2, jnp.where(ise, jnp.where(hit1, 1e9, lanef), 1e9), 1e9),
                 axis=1, keepdims=True)
    hit2 = lanef == i2
    den = p1 + p2
    comb_ref[...] = jnp.where(hit1, g_w * (p1 / den), jnp.where(hit2, g_w * (p2 / den), 0.0))


def _outproj(x, yab, yc, w_out, g, b, wr, br, tm):
    n = x.shape[0]
    full = lambda i: (0, 0)
    wr_hi = wr.astype(BF16)
    wr_lo = (wr - wr_hi.astype(F32)).astype(BF16)
    return pl.pallas_call(
        _outproj_kernel,
        grid=(n // tm,),
        in_specs=[pl.BlockSpec((tm, D_MODEL), lambda i: (i, 0)),
                  pl.BlockSpec((tm, 512), lambda i: (i, 0)),
                  pl.BlockSpec((tm, 512), lambda i: (i, 0)),
                  pl.BlockSpec((D_MODEL, D_MODEL), full),
                  pl.BlockSpec((1, D_MODEL), full), pl.BlockSpec((1, D_MODEL), full),
                  pl.BlockSpec((D_MODEL, 128), full), pl.BlockSpec((D_MODEL, 128), full),
                  pl.BlockSpec((1, 128), full)],
        out_specs=[pl.BlockSpec((tm, D_MODEL), lambda i: (i, 0)),
                   pl.BlockSpec((tm, 128), lambda i: (i, 0))],
        out_shape=[jax.ShapeDtypeStruct((n, D_MODEL), F32), jax.ShapeDtypeStruct((n, 128), F32)],
        compiler_params=_cparams(("arbitrary",)),
        name="outproj_ln_route",
    )(x, yab, yc, w_out, g, b, wr_hi, wr_lo, br)


def _moe_kernel(x_ref, comb_ref, wgu_ref, wd_ref, g_ref, b_ref, o_ref, xb_ref, acc_ref):
    e = pl.program_id(1)

    @pl.when(e == 0)
    def _():
        xb_ref[...] = x_ref[...].astype(BF16)
        acc_ref[...] = jnp.zeros(acc_ref.shape, F32)

    gu = _dot(xb_ref[...], wgu_ref[0])
    hid = _silu(gu[:, 0:D_EXPERT]) * gu[:, D_EXPERT:2 * D_EXPERT]
    comb = comb_ref[...]
    lane = lax.broadcasted_iota(jnp.int32, comb.shape, 1)
    c = jnp.sum(jnp.where(lane == e, comb, 0.0), axis=1, keepdims=True)
    acc_ref[...] += _dot((hid * c).astype(BF16), wd_ref[0])

    @pl.when(e == pl.num_programs(1) - 1)
    def _():
        o_ref[...] = _ln(ALPHA * x_ref[...] + acc_ref[...], g_ref[...], b_ref[...])


def _moe(x1, comb, wgu, wd, g, b, tm):
    n = x1.shape[0]
    return pl.pallas_call(
        _moe_kernel,
        grid=(n // tm, N_EXPERTS),
        in_specs=[pl.BlockSpec((tm, D_MODEL), lambda i, e: (i, 0)),
                  pl.BlockSpec((tm, 128), lambda i, e: (i, 0)),
                  pl.BlockSpec((1, D_MODEL, 2 * D_EXPERT), lambda i, e: (e, 0, 0)),
                  pl.BlockSpec((1, D_EXPERT, D_MODEL), lambda i, e: (e, 0, 0)),
                  pl.BlockSpec((1, D_MODEL), lambda i, e: (0, 0)),
                  pl.BlockSpec((1, D_MODEL), lambda i, e: (0, 0))],
        out_specs=pl.BlockSpec((tm, D_MODEL), lambda i, e: (i, 0)),
        out_shape=jax.ShapeDtypeStruct((n, D_MODEL), F32),
        scratch_shapes=[pltpu.VMEM((tm, D_MODEL), BF16), pltpu.VMEM((tm, D_MODEL), F32)],
        compiler_params=_cparams(("arbitrary", "arbitrary")),
        name="moe_ffn_ln",
    )(x1, comb, wgu, wd, g, b)


def _sinproj_kernel(*refs, apply_ln):
    if apply_ln:
        (x_ref, lng_ref, lnb_ref, w_ref, caw_ref, cab_ref, calg_ref, calb_ref, cbw_ref, sta_ref, stb_ref,
         h_ref, q_ref, k_ref, v_ref, yab_ref, u_ref, ch_ref) = refs
    else:
        (x_ref, w_ref, caw_ref, cab_ref, calg_ref, calb_ref, cbw_ref, sta_ref, stb_ref,
         q_ref, k_ref, v_ref, yab_ref, u_ref, ch_ref) = refs
    x = x_ref[...]
    if apply_ln:
        x = _ln(x, lng_ref[...], lnb_ref[...])
        h_ref[...] = x
    xb = x.astype(BF16)

    def seg(a, b):
        return _dot(xb, w_ref[:, a:b])

    u = seg(0, 256) * jax.nn.sigmoid(seg(256, 512))
    u_ref[...] = u
    acc = caw_ref[CONF_WIDTH - 1:CONF_WIDTH, :] * u
    for j in range(CONF_WIDTH - 1):
        acc = acc + caw_ref[j:j + 1, :] * sta_ref[0, j]
    y_a = _silu(_ln(acc + cab_ref[...], calg_ref[...], calb_ref[...]))
    yab_ref[:, 0:256] = y_a.astype(BF16)

    ch = seg(768, 1024) * seg(1024, 1280)
    ch_ref[...] = ch
    accb = cbw_ref[SCONV_WIDTH - 1:SCONV_WIDTH, :] * ch
    for j in range(SCONV_WIDTH - 1):
        accb = accb + cbw_ref[j:j + 1, :] * stb_ref[0, j]
    yab_ref[:, 256:512] = (seg(512, 768) * accb).astype(BF16)

    q_ref[...] = seg(1280, 1792) * ATT_SCALE
    k_ref[...] = seg(1792, 2304)
    v_ref[...] = seg(2304, 2816)


def _sinproj(x, ln_g, ln_b, w_in, caw, cab, calg, calb, cbw, sta_t, stb_t, layer, apply_ln):
    n = x.shape[0]
    full = lambda i: (0, 0)
    in_specs = [pl.BlockSpec((n, D_MODEL), full)]
    args = [x]
    if apply_ln:
        in_specs += [pl.BlockSpec((1, D_MODEL), full), pl.BlockSpec((1, D_MODEL), full)]
        args += [ln_g, ln_b]
    in_specs += [pl.BlockSpec((D_MODEL, D_IN), full),
                 pl.BlockSpec((CONF_WIDTH, D_CONF), full),
                 pl.BlockSpec((1, D_CONF), full), pl.BlockSpec((1, D_CONF), full), pl.BlockSpec((1, D_CONF), full),
                 pl.BlockSpec((SCONV_WIDTH, D_SCONV), full),
                 pl.BlockSpec((1, CONF_WIDTH - 1, n, D_CONF), lambda i: (layer, 0, 0, 0)),
                 pl.BlockSpec((1, SCONV_WIDTH - 1, n, D_SCONV), lambda i: (layer, 0, 0, 0))]
    args += [w_in, caw, cab, calg, calb, cbw, sta_t, stb_t]
    out_specs, out_shape = [], []
    if apply_ln:
        out_specs.append(pl.BlockSpec((n, D_MODEL), full))
        out_shape.append(jax.ShapeDtypeStruct((n, D_MODEL), F32))
    out_specs += [pl.BlockSpec((n, D_ATT), full)] * 3 + [pl.BlockSpec((n, 512), full),
                                                        pl.BlockSpec((n, D_CONF), full),
                                                        pl.BlockSpec((n, D_SCONV), full)]
    out_shape += [jax.ShapeDtypeStruct((n, D_ATT), F32)] * 3 + [jax.ShapeDtypeStruct((n, 512), BF16),
                                                               jax.ShapeDtypeStruct((n, D_CONF), F32),
                                                               jax.ShapeDtypeStruct((n, D_SCONV), F32)]
    return pl.pallas_call(
        functools.partial(_sinproj_kernel, apply_ln=apply_ln),
        grid=(1,),
        in_specs=in_specs, out_specs=out_specs, out_shape=out_shape,
        compiler_params=_cparams(("arbitrary",)),
        name="sample_inproj_conv",
    )(*args)


N_PAGES = 16
N_PAST_BLK = N_PAGES * PAGE // BLK
SEL_STRIDE = 16


def _page_scores(kt_page, qcol):
    return jnp.sum(kt_page * qcol, axis=0, keepdims=True)


def _sgate_kernel(pt_ref, qt_ref, ptv_ref, *refs):
    krefs = refs[:N_PAGES]
    out_ref, sc_ref = refs[N_PAGES], refs[N_PAGES + 1]
    lane1 = lax.broadcasted_iota(jnp.int32, (1, LANES), 1)
    rows = []
    for hd in range(N_HEADS):
        qcol = jnp.broadcast_to(qt_ref[0, :, hd:hd + 1], (HEAD_DIM, PAGE))
        gh = jnp.zeros((1, LANES), F32)
        for n in range(N_PAST_BLK):
            sc0 = _page_scores(krefs[2 * n][0, 0, hd], qcol)
            sc1 = _page_scores(krefs[2 * n + 1][0, 0, hd], qcol)
            sc_ref[0, hd, 2 * n:2 * n + 1, :] = sc0
            sc_ref[0, hd, 2 * n + 1:2 * n + 2, :] = sc1
            gh = jnp.where(lane1 == n, jnp.sum(sc0 + sc1, axis=1, keepdims=True), gh)
        rows.append(gh)
    g = jnp.concatenate(rows, axis=0)
    lane = lax.broadcasted_iota(jnp.int32, (N_HEADS, LANES), 1)
    lanef = lane.astype(F32)
    g = jnp.where(lane < N_PAST_BLK, g, LOWEST)
    ptf = ptv_ref[0].astype(F32)
    l16 = lax.broadcasted_iota(jnp.int32, (N_HEADS, N_PAGES), 1).astype(F32)
    res = jnp.zeros((N_HEADS, LANES), F32)
    for r in range(TOPK):
        mx = jnp.max(g, axis=1, keepdims=True)
        idx = jnp.min(jnp.where(g == mx, lanef, 1e9), axis=1, keepdims=True)
        g = jnp.where(lanef == idx, LOWEST, g)
        pg0 = jnp.sum(jnp.where(l16 == 2.0 * idx, ptf, 0.0), axis=1, keepdims=True)
        pg1 = jnp.sum(jnp.where(l16 == 2.0 * idx + 1.0, ptf, 0.0), axis=1, keepdims=True)
        res = jnp.where(lane == 2 * r, pg0, res)
        res = jnp.where(lane == 2 * r + 1, pg1, res)
        res = jnp.where(lane == 8 + r, idx, res)
    out_ref[0] = res.astype(jnp.int32)


def _sgate(page_table, qt8, cache_kt, layer):
    nseq = page_table.shape[0]
    ptv = page_table.reshape(nseq, 1, N_PAGES)

    def kspec(j):
        return pl.BlockSpec((1, 1, N_HEADS, HEAD_DIM, PAGE), lambda b, pt: (layer, pt[b * N_PAGES + j], 0, 0, 0))

    grid_spec = pltpu.PrefetchScalarGridSpec(
        num_scalar_prefetch=1,
        grid=(nseq,),
        in_specs=[pl.BlockSpec((1, HEAD_DIM, N_HEADS), lambda b, pt: (b, 0, 0)),
                  pl.BlockSpec((1, 1, N_PAGES), lambda b, pt: (b, 0, 0))]
                 + [kspec(j) for j in range(N_PAGES)],
        out_specs=[pl.BlockSpec((1, N_HEADS, LANES), lambda b, pt: (b, 0, 0)),
                   pl.BlockSpec((1, N_HEADS, N_PAGES, PAGE), lambda b, pt: (b, 0, 0, 0))],
    )
    return pl.pallas_call(
        _sgate_kernel,
        grid_spec=grid_spec,
        out_shape=[jax.ShapeDtypeStruct((nseq, N_HEADS, LANES), jnp.int32),
                   jax.ShapeDtypeStruct((nseq, N_HEADS, N_PAGES, PAGE), F32)],
        compiler_params=_cparams(("arbitrary",)),
        name="sample_block_gate",
    )(page_table.reshape(-1), qt8, ptv, *([cache_kt] * N_PAGES))


N_SEL_PAGES = TOPK * BLK // PAGE


def _sattn_kernel(sel_ref, rb_ref, q_ref, kn_ref, vnt_ref, stab_ref, sc_ref, *refs):
    np_ = N_HEADS * N_SEL_PAGES
    vrefs = refs[:np_]
    o_ref = refs[np_]
    b = pl.program_id(0)
    s_self_all = jnp.sum(q_ref[0] * kn_ref[0], axis=1, keepdims=True)
    for hd in range(N_HEADS):
        base = (b * N_HEADS + hd) * SEL_STRIDE
        c31 = rb_ref[N_BUCKETS - 1, hd]
        scores = []
        for s in range(N_SEL_PAGES):
            blk = sel_ref[base + 8 + s // 2]
            sc = sc_ref[0, hd, pl.ds(2 * blk + s % 2, 1), :]
            tab = stab_ref[hd, (s % 2):(s % 2) + 1, :]
            scores.append(sc + jnp.where(blk == N_PAST_BLK - 1, tab, c31))
        s_self = s_self_all[hd:hd + 1, :] + rb_ref[0, hd]
        m = s_self
        for sc in scores:
            m = jnp.maximum(m, jnp.max(sc, axis=1, keepdims=True))
        e_self = jnp.exp(s_self - m)
        den = e_self
        acc = jnp.zeros((HEAD_DIM, PAGE), F32)
        for s in range(N_SEL_PAGES):
            e = jnp.exp(scores[s] - m)
            den = den + jnp.sum(e, axis=1, keepdims=True)
            acc = acc + vrefs[hd * N_SEL_PAGES + s][0, 0, 0] * e
        num = jnp.sum(acc, axis=1, keepdims=True) + e_self * vnt_ref[0, :, hd:hd + 1]
        o_ref[0, :, hd:hd + 1] = num / den


def _sattn(sel, rel_bias, q8, kn8, vnt8, stab, scores, cache_vt, layer):
    nseq = q8.shape[0]

    def pspec(hd, s):
        return pl.BlockSpec((1, 1, 1, HEAD_DIM, PAGE),
                            lambda b, sl: (layer, sl[(b * N_HEADS + hd) * SEL_STRIDE + s], hd, 0, 0))

    def page_specs():
        return [pspec(hd, s) for hd in range(N_HEADS) for s in range(N_SEL_PAGES)]

    def row():
        return pl.BlockSpec((1, N_HEADS, HEAD_DIM), lambda b, sl: (b, 0, 0))

    def col():
        return pl.BlockSpec((1, HEAD_DIM, N_HEADS), lambda b, sl: (b, 0, 0))

    grid_spec = pltpu.PrefetchScalarGridSpec(
        num_scalar_prefetch=1,
        grid=(nseq,),
        in_specs=[pl.BlockSpec(memory_space=pltpu.SMEM), row(), row(), col(),
                  pl.BlockSpec((N_HEADS, 8, PAGE), lambda b, sl: (0, 0, 0)),
                  pl.BlockSpec((1, N_HEADS, N_PAGES, PAGE), lambda b, sl: (b, 0, 0, 0))] + page_specs(),
        out_specs=col(),
    )
    npg = N_HEADS * N_SEL_PAGES
    return pl.pallas_call(
        _sattn_kernel,
        grid_spec=grid_spec,
        out_shape=jax.ShapeDtypeStruct((nseq, HEAD_DIM, N_HEADS), F32),
        compiler_params=_cparams(("arbitrary",)),
        name="sample_attention",
    )(sel.reshape(-1), rel_bias, q8, kn8, vnt8, stab, scores, *([cache_vt] * npg))


TM_OUT = 512
TM_MOE = 1024


def kernel(x_prompt, x_sample, cache_k, cache_v, state_conv_a, state_conv_b, page_table, ln_in_g, ln_in_b, w_in, conv_a_w, conv_a_b, conv_a_ln_g, conv_a_ln_b, conv_b_w, rel_bias, w_out, ln1_g, ln1_b, w_group, b_group, w_router, b_router, w_gate, w_up, w_down, ln2_g, ln2_b):
    bsz, t, _ = x_prompt.shape
    nseq = x_sample.shape[0]
    n_p = bsz * t
    row = lambda a: a.reshape(1, -1)

    btab, stab = _bias_tables(rel_bias)
    sta_t = state_conv_a.transpose(0, 2, 1, 3)
    stb_t = state_conv_b.transpose(0, 2, 1, 3)
    cache_kt = cache_k.transpose(0, 1, 2, 4, 3)
    cache_vt = cache_v.transpose(0, 1, 2, 4, 3)
    x_s = x_sample.reshape(nseq, D_MODEL)
    kpages = jnp.zeros((DEPTH, bsz, t // PAGE, N_HEADS, HEAD_DIM, PAGE), F32)
    vpages = jnp.zeros((DEPTH, bsz, t // PAGE, N_HEADS, HEAD_DIM, PAGE), F32)

    h_p = x_prompt
    h_s = x_s
    ap_l, bp_l, ks_l, vs_l, us_l, cs_l = [], [], [], [], [], []
    for l in range(DEPTH):
        first = l == 0
        w_in_b = w_in[l].astype(BF16)
        w_out_b = w_out[l].astype(BF16)
        wgu = jnp.concatenate([w_gate[l], w_up[l]], axis=-1).astype(BF16)
        wd = w_down[l].astype(BF16)
        wr = jnp.pad(jnp.concatenate([w_router[l], w_group[l]], axis=-1), ((0, 0), (0, 128 - N_EXPERTS - N_GROUPS)))
        br = jnp.pad(jnp.concatenate([b_router[l], b_group[l]]), (0, 128 - N_EXPERTS - N_GROUPS)).reshape(1, 128)
        conv_args = (conv_a_w[l], row(conv_a_b[l]), row(conv_a_ln_g[l]), row(conv_a_ln_b[l]), conv_b_w[l])

        outs = _inproj(h_p, row(ln_in_g), row(ln_in_b), w_in_b, *conv_args, kpages, vpages, l, apply_ln=first)
        if first:
            h_p, outs = outs[0], outs[1:]
        qa, kt, vb, kpages, vpages, yab, new_a, new_b = outs
        yc = _attention(qa, kt, vb, btab)
        x1, comb = _outproj(h_p.reshape(n_p, D_MODEL), yab.reshape(n_p, 512), yc.reshape(n_p, D_ATT),
                            w_out_b, row(ln1_g[l]), row(ln1_b[l]), wr, br, TM_OUT)
        h_p = _moe(x1, comb, wgu, wd, row(ln2_g[l]), row(ln2_b[l]), TM_MOE).reshape(bsz, t, D_MODEL)
        ap_l.append(new_a)
        bp_l.append(new_b)

        souts = _sinproj(h_s, row(ln_in_g), row(ln_in_b), w_in_b, *conv_args, sta_t, stb_t, l, apply_ln=first)
        if first:
            h_s, souts = souts[0], souts[1:]
        q_s, k_s, v_s, yab_s, u_s, ch_s = souts
        q8 = q_s.reshape(nseq, N_HEADS, HEAD_DIM)
        k8 = k_s.reshape(nseq, N_HEADS, HEAD_DIM)
        v8 = v_s.reshape(nseq, N_HEADS, HEAD_DIM)
        qt8 = q8.transpose(0, 2, 1)
        sel, scores_s = _sgate(page_table, qt8, cache_kt, l)
        yc_s = _sattn(sel[:, :, :SEL_STRIDE], rel_bias, q8, k8, v8.transpose(0, 2, 1), stab, scores_s, cache_vt, l)
        yc_s = yc_s.transpose(0, 2, 1).reshape(nseq, D_ATT).astype(BF16)
        x1_s, comb_s = _outproj(h_s, yab_s, yc_s, w_out_b, row(ln1_g[l]), row(ln1_b[l]), wr, br, nseq)
        h_s = _moe(x1_s, comb_s, wgu, wd, row(ln2_g[l]), row(ln2_b[l]), nseq)
        ks_l.append(k8.reshape(nseq, N_HEADS, 1, HEAD_DIM))
        vs_l.append(v8.reshape(nseq, N_HEADS, 1, HEAD_DIM))
        us_l.append(u_s)
        cs_l.append(ch_s)

    new_a_s = jnp.concatenate([state_conv_a[:, :, 1:], jnp.stack(us_l)[:, :, None, :]], axis=2)
    new_b_s = jnp.concatenate([state_conv_b[:, :, 1:], jnp.stack(cs_l)[:, :, None, :]], axis=2)
    return (h_p, h_s.reshape(nseq, 1, D_MODEL),
            kpages.transpose(0, 1, 2, 3, 5, 4), vpages.transpose(0, 1, 2, 3, 5, 4),
            jnp.stack(ap_l), jnp.stack(bp_l), jnp.stack(ks_l), jnp.stack(vs_l), new_a_s, new_b_s)
```

```python
import functools
import math

import numpy as np
import jax
import jax.numpy as jnp
from jax import lax
from jax.experimental import pallas as pl
from jax.experimental.pallas import tpu as pltpu

D_MODEL = 1024
HEAD_DIM = 64
D_CONF = 256
D_SCONV = 256
D_ATT = 512
N_HEADS = 8
CONF_WIDTH = 31
SCONV_WIDTH = 3
PAGE = 128
BLK = 256
TOPK = 3
N_BUCKETS = 32
MAX_DISTANCE = 128
N_GROUPS = 4
EPG = 4
N_EXPERTS = 16
D_EXPERT = 256
DEPTH = 4
ALPHA = (2 * DEPTH) ** 0.25
LN_EPS = 1e-5
NEG = -1e30
LOWEST = -3e38
ATT_SCALE = HEAD_DIM ** -0.5
LOG2E = math.log2(math.e)
D_IN = 2 * D_CONF + 3 * D_SCONV + 3 * D_ATT
LANES = 128
PEN_LANES = 16

F32 = jnp.float32
BF16 = jnp.bfloat16
HIGHEST = lax.Precision.HIGHEST

VMEM_LIMIT = 56 * 1024 * 1024


def _cparams(sem):
    return pltpu.CompilerParams(dimension_semantics=sem, vmem_limit_bytes=VMEM_LIMIT)


def _ln(x, g, b):
    mu = jnp.mean(x, axis=-1, keepdims=True)
    xc = x - mu
    var = jnp.mean(xc * xc, axis=-1, keepdims=True)
    return xc * lax.rsqrt(var + LN_EPS) * g + b


def _silu(x):
    return x * jax.nn.sigmoid(x)


def _dot(a, b):
    return jnp.dot(a, b, preferred_element_type=F32)


def _gate_row(head, n):
    return HEAD_DIM * (1 - head % 2) + PEN_LANES * (head // 2) + n


def _top3_rows(gate, rowf, valid, axis=0):
    g = jnp.where(valid, gate, NEG)
    sel = jnp.zeros(gate.shape, F32)
    for _ in range(TOPK):
        mx = jnp.max(g, axis=axis, keepdims=True)
        idx = jnp.min(jnp.where(g == mx, rowf, 1e9), axis=axis, keepdims=True)
        hit = rowf == idx
        sel = jnp.where(hit, 1.0, sel)
        g = jnp.where(hit, LOWEST, g)
    return jnp.where(valid, sel, 0.0)


def _t5_bucket_np(dist):
    n = np.maximum(dist, 0)
    max_exact = N_BUCKETS // 2
    nf = np.maximum(n, 1).astype(np.float32)
    large = max_exact + (np.log(nf / np.float32(max_exact)) / np.float32(math.log(MAX_DISTANCE / max_exact))
                         * np.float32(N_BUCKETS - max_exact)).astype(np.int32)
    large = np.minimum(large, N_BUCKETS - 1)
    return np.where(n < max_exact, n, large).astype(np.int32)


def _bias_kernel(rb_ref, idx0_ref, idx1_ref, idxs_ref, btab_ref, stab_ref):
    h = pl.program_id(0)
    idx0 = idx0_ref[...]
    idx1 = idx1_ref[...]
    idxs = idxs_ref[...]
    t0 = jnp.zeros(idx0.shape, F32)
    t1 = jnp.zeros(idx1.shape, F32)
    ts = jnp.zeros(idxs.shape, F32)
    for bkt in range(N_BUCKETS):
        val = rb_ref[bkt, h]
        t0 = jnp.where(idx0 == bkt, val, t0)
        t1 = jnp.where(idx1 == bkt, val, t1)
        ts = jnp.where(idxs == bkt, val, ts)
    far = rb_ref[N_BUCKETS - 1, h]
    btab_ref[0, 0] = (t0 - far) * LOG2E
    btab_ref[0, 1] = (t1 - far) * LOG2E
    stab_ref[0] = ts


def _bias_tables(rel_bias):
    r = np.arange(BLK)[:, None]
    c = np.arange(BLK)[None, :]
    idx0 = _t5_bucket_np(r - c)
    idx1 = _t5_bucket_np(BLK + r - c)
    idxs = np.zeros((8, PAGE), np.int32)
    idxs[0:2] = _t5_bucket_np(BLK - np.arange(BLK)).reshape(2, PAGE)
    return pl.pallas_call(
        _bias_kernel,
        grid=(N_HEADS,),
        in_specs=[pl.BlockSpec(memory_space=pltpu.SMEM),
                  pl.BlockSpec((BLK, BLK), lambda h: (0, 0)),
                  pl.BlockSpec((BLK, BLK), lambda h: (0, 0)),
                  pl.BlockSpec((8, PAGE), lambda h: (0, 0))],
        out_specs=[pl.BlockSpec((1, 2, BLK, BLK), lambda h: (h, 0, 0, 0)),
                   pl.BlockSpec((1, 8, PAGE), lambda h: (h, 0, 0))],
        out_shape=[jax.ShapeDtypeStruct((N_HEADS, 2, BLK, BLK), F32),
                   jax.ShapeDtypeStruct((N_HEADS, 8, PAGE), F32)],
        compiler_params=_cparams(("arbitrary",)),
        name="bias_tables",
    )(rel_bias, jnp.asarray(idx0), jnp.asarray(idx1), jnp.asarray(idxs))


TM1 = BLK


def _inproj_kernel(*refs, apply_ln):
    if apply_ln:
        (x_ref, lng_ref, lnb_ref, w_ref, caw_ref, cab_ref, calg_ref, calb_ref, cbw_ref, _kin, _vin,
         h_ref, qa_ref, kt_ref, vb_ref, kpg_ref, vpg_ref, yab_ref, newa_ref, newb_ref,
         ua_ref, ub_ref, ksh_ref) = refs
    else:
        (x_ref, w_ref, caw_ref, cab_ref, calg_ref, calb_ref, cbw_ref, _kin, _vin,
         qa_ref, kt_ref, vb_ref, kpg_ref, vpg_ref, yab_ref, newa_ref, newb_ref,
         ua_ref, ub_ref, ksh_ref) = refs
    ti = pl.program_id(1)
    last = pl.num_programs(1) - 1
    x = x_ref[0]
    if apply_ln:
        x = _ln(x, lng_ref[...], lnb_ref[...])
        h_ref[0] = x
    xb = x.astype(BF16)

    def seg(a, b):
        return _dot(xb, w_ref[:, a:b])

    @pl.when(ti == 0)
    def _():
        ua_ref[0:32, :] = jnp.zeros((32, D_CONF), F32)
        ub_ref[0:8, :] = jnp.zeros((8, D_SCONV), F32)
        ksh_ref[...] = jnp.zeros(ksh_ref.shape, F32)

    u = seg(0, 256) * jax.nn.sigmoid(seg(256, 512))
    ua_ref[32:32 + TM1, :] = u
    acc = jnp.zeros((TM1, D_CONF), F32)
    for j in range(CONF_WIDTH):
        acc = acc + caw_ref[j:j + 1, :] * ua_ref[2 + j:2 + j + TM1, :]
    y_a = _silu(_ln(acc + cab_ref[...], calg_ref[...], calb_ref[...]))
    yab_ref[0, :, 0:256] = y_a.astype(BF16)

    ch = seg(768, 1024) * seg(1024, 1280)
    ub_ref[8:8 + TM1, :] = ch
    accb = jnp.zeros((TM1, D_SCONV), F32)
    for j in range(SCONV_WIDTH):
        accb = accb + cbw_ref[j:j + 1, :] * ub_ref[6 + j:6 + j + TM1, :]
    yab_ref[0, :, 256:512] = (seg(512, 768) * accb).astype(BF16)

    @pl.when(ti == last)
    def _():
        newa_ref[0] = ua_ref[TM1 + 2:TM1 + 32, :]
        newb_ref[0] = ub_ref[TM1 + 6:TM1 + 8, :]

    ua_ref[0:32, :] = ua_ref[TM1:TM1 + 32, :]
    ub_ref[0:8, :] = ub_ref[TM1:TM1 + 8, :]

    q = seg(1280, 1792) * (ATT_SCALE * LOG2E)
    k = seg(1792, 2304)
    v = seg(2304, 2816)

    ks = ksh_ref[...]
    ks_hi = ks.astype(BF16)
    ks_lo = (ks - ks_hi.astype(F32)).astype(BF16)
    q_hi = q.astype(BF16)
    q_lo = (q - q_hi.astype(F32)).astype(BF16)
    nt_dims = (((1,), (1,)), ((), ()))
    gate = (lax.dot_general(ks_hi, q_hi, nt_dims, preferred_element_type=F32)
            + (lax.dot_general(ks_lo, q_hi, nt_dims, preferred_element_type=F32)
               + lax.dot_general(ks_hi, q_lo, nt_dims, preferred_element_type=F32)))
    g3 = gate.reshape(N_HEADS, PEN_LANES, TM1)
    rown3 = lax.broadcasted_iota(jnp.int32, g3.shape, 1)
    sel3 = _top3_rows(g3, rown3.astype(F32), rown3 < ti, axis=1)
    pen_all = jnp.where(sel3 > 0.5, 0.0, NEG).reshape(N_HEADS * PEN_LANES, TM1).T
    lane = lax.broadcasted_iota(jnp.int32, (TM1, LANES), 1)
    for hd in range(N_HEADS):
        hh, pair = hd % 2, hd // 2
        in_q = (lane >= HEAD_DIM * hh) & (lane < HEAD_DIM * (hh + 1))
        qa_ref[0, hd] = jnp.where(in_q, q[:, pair * LANES:(pair + 1) * LANES], pen_all).astype(BF16)
    ksum = jnp.sum(k, axis=0, keepdims=True)
    col = lax.broadcasted_iota(jnp.int32, (1, D_ATT), 1)
    for hd in range(N_HEADS):
        in_h = (col >= HEAD_DIM * hd) & (col < HEAD_DIM * (hd + 1))
        ksh_ref[pl.ds(_gate_row(hd, ti), 1), :] = jnp.where(in_h, ksum, 0.0)

    kt = k.T
    vt = v.T
    kt_ref[0, 0] = kt.astype(BF16)
    vb_ref[0] = v.astype(BF16)
    for p in range(TM1 // PAGE):
        for hd in range(N_HEADS):
            kpg_ref[0, 0, p, hd] = kt[hd * HEAD_DIM:(hd + 1) * HEAD_DIM, p * PAGE:(p + 1) * PAGE]
            vpg_ref[0, 0, p, hd] = vt[hd * HEAD_DIM:(hd + 1) * HEAD_DIM, p * PAGE:(p + 1) * PAGE]


def _inproj(x, ln_g, ln_b, w_in, caw, cab, calg, calb, cbw, kpages, vpages, layer, apply_ln):
    bsz, t, _ = x.shape
    nt = t // TM1
    full2 = lambda b, i: (0, 0)
    in_specs = [pl.BlockSpec((1, TM1, D_MODEL), lambda b, i: (b, i, 0))]
    args = [x]
    if apply_ln:
        in_specs += [pl.BlockSpec((1, D_MODEL), full2), pl.BlockSpec((1, D_MODEL), full2)]
        args += [ln_g, ln_b]
    in_specs += [pl.BlockSpec((D_MODEL, D_IN), full2),
                 pl.BlockSpec((CONF_WIDTH, D_CONF), full2),
                 pl.BlockSpec((1, D_CONF), full2), pl.BlockSpec((1, D_CONF), full2),
                 pl.BlockSpec((1, D_CONF), full2),
                 pl.BlockSpec((SCONV_WIDTH, D_SCONV), full2),
                 pl.BlockSpec(memory_space=pl.ANY), pl.BlockSpec(memory_space=pl.ANY)]
    args += [w_in, caw, cab, calg, calb, cbw, kpages, vpages]
    kpages_in = len(args) - 2
    n_extra = 1 if apply_ln else 0
    page_block = (1, 1, TM1 // PAGE, N_HEADS, HEAD_DIM, PAGE)
    page_map = lambda b, i: (layer, b, i, 0, 0, 0)
    out_specs, out_shape = [], []
    if apply_ln:
        out_specs.append(pl.BlockSpec((1, TM1, D_MODEL), lambda b, i: (b, i, 0)))
        out_shape.append(jax.ShapeDtypeStruct((bsz, t, D_MODEL), F32))
    out_specs += [
        pl.BlockSpec((1, N_HEADS, TM1, LANES), lambda b, i: (b, 0, i, 0)),
        pl.BlockSpec((1, 1, D_ATT, TM1), lambda b, i: (b, i, 0, 0)),
        pl.BlockSpec((1, TM1, D_ATT), lambda b, i: (b, i, 0)),
        pl.BlockSpec(page_block, page_map),
        pl.BlockSpec(page_block, page_map),
        pl.BlockSpec((1, TM1, 512), lambda b, i: (b, i, 0)),
        pl.BlockSpec((1, CONF_WIDTH - 1, D_CONF), lambda b, i: (b, 0, 0)),
        pl.BlockSpec((1, SCONV_WIDTH - 1, D_SCONV), lambda b, i: (b, 0, 0)),
    ]
    out_shape += [
        jax.ShapeDtypeStruct((bsz, N_HEADS, t, LANES), BF16),
        jax.ShapeDtypeStruct((bsz, nt, D_ATT, TM1), BF16),
        jax.ShapeDtypeStruct((bsz, t, D_ATT), BF16),
        jax.ShapeDtypeStruct(kpages.shape, F32),
        jax.ShapeDtypeStruct(vpages.shape, F32),
        jax.ShapeDtypeStruct((bsz, t, 512), BF16),
        jax.ShapeDtypeStruct((bsz, CONF_WIDTH - 1, D_CONF), F32),
        jax.ShapeDtypeStruct((bsz, SCONV_WIDTH - 1, D_SCONV), F32),
    ]
    assert nt <= PEN_LANES
    return pl.pallas_call(
        functools.partial(_inproj_kernel, apply_ln=apply_ln),
        grid=(bsz, nt),
        in_specs=in_specs, out_specs=out_specs, out_shape=out_shape,
        input_output_aliases={kpages_in: n_extra + 3, kpages_in + 1: n_extra + 4},
        scratch_shapes=[pltpu.VMEM((32 + TM1, D_CONF), F32), pltpu.VMEM((8 + TM1, D_SCONV), F32),
                        pltpu.VMEM((N_HEADS * PEN_LANES, D_ATT), F32)],
        compiler_params=_cparams(("arbitrary", "arbitrary")),
        name="inproj_conv",
    )(*args)


def _attn_kernel(qa_ref, kt_ref, v_ref, btab_ref, o_ref, m_ref, l_ref, acc_ref, s_ref):
    i = pl.program_id(2)
    nb = kt_ref.shape[1]
    lane = lax.broadcasted_iota(jnp.int32, (BLK, LANES), 1)
    rowd = lax.broadcasted_iota(jnp.int32, (LANES, BLK), 0)
    rq = lax.broadcasted_iota(jnp.int32, (BLK, BLK), 0)
    ck = lax.broadcasted_iota(jnp.int32, (BLK, BLK), 1)

    qa, head_rows, pen_base = [], [], []
    for hh in range(2):
        in_t = (rowd >= HEAD_DIM * hh) & (rowd < HEAD_DIM * (hh + 1))
        qa.append(qa_ref[0, hh])
        head_rows.append(jnp.where(in_t, 1.0, 0.0).astype(BF16))
        pen_base.append(HEAD_DIM * (1 - hh) + PEN_LANES * pl.program_id(1))
        m_ref[hh] = jnp.full((BLK, LANES), LOWEST, F32)
        l_ref[hh] = jnp.zeros((BLK, LANES), F32)
        acc_ref[hh] = jnp.zeros((BLK, LANES), F32)

    def scores(hh, ktn, n, kind):
        wk = ktn * head_rows[hh]
        if kind == "own":
            return jnp.where(ck <= rq, _dot(qa[hh], wk) + btab_ref[hh, 0], NEG)
        wk = wk + jnp.where(rowd == pen_base[hh] + n, 1.0, 0.0).astype(BF16)
        s = _dot(qa[hh], wk)
        if kind == "prev":
            s = s + btab_ref[hh, 1]
        return s

    def pass1(blocks):
        kts = [kt_ref[0, n] for n, _, _ in blocks]
        for hh in range(2):
            mx = m_ref[hh]
            for (n, kind, slot), ktn in zip(blocks, kts):
                s = scores(hh, ktn, n, kind)
                s_ref[hh, slot] = s
                mx = jnp.maximum(mx, jnp.maximum(s[:, :LANES], s[:, LANES:]))
            m_ref[hh] = mx

    def pass2(blocks):
        vs = [v_ref[0, pl.ds(pl.multiple_of(n * BLK, BLK), BLK), :] for n, _ in blocks]
        for hh in range(2):
            mb = m_ref[hh]
            lsum = l_ref[hh]
            osum = acc_ref[hh]
            for (n, slot), vn in zip(blocks, vs):
                p0 = jnp.exp2(s_ref[hh, slot, :, :LANES] - mb)
                p1 = jnp.exp2(s_ref[hh, slot, :, LANES:] - mb)
                lsum = lsum + (p0 + p1)
                osum = osum + _dot(jnp.concatenate([p0, p1], axis=1).astype(BF16), vn)
            l_ref[hh] = lsum
            acc_ref[hh] = osum

    n_far = jnp.maximum(i - 1, 0)
    n_prev = jnp.maximum(i - 1, 0)
    own_slot = nb
    n_pairs = n_far // 2
    odd = n_far - 2 * n_pairs

    def far1(j, c):
        pass1([(2 * j, "far", 2 * j), (2 * j + 1, "far", 2 * j + 1)])
        return c

    def far2(j, c):
        pass2([(2 * j, 2 * j), (2 * j + 1, 2 * j + 1)])
        return c

    lax.fori_loop(0, n_pairs, far1, 0)

    @pl.when(odd == 1)
    def _():
        pass1([(n_far - 1, "far", n_far - 1)])

    pass1([(n_prev, "prev", n_prev), (i, "own", own_slot)])
    for hh in range(2):
        m_ref[hh] = jnp.broadcast_to(jnp.max(m_ref[hh], axis=1, keepdims=True), (BLK, LANES))
    lax.fori_loop(0, n_pairs, far2, 0)

    @pl.when(odd == 1)
    def _():
        pass2([(n_far - 1, n_far - 1)])

    pass2([(n_prev, n_prev), (i, own_slot)])

    o0 = acc_ref[0] / jnp.sum(l_ref[0], axis=1, keepdims=True)
    o1 = acc_ref[1] / jnp.sum(l_ref[1], axis=1, keepdims=True)
    o_ref[0] = jnp.where(lane < HEAD_DIM, o0, o1).astype(BF16)


def _attention(qa, kt, vb, btab):
    bsz, t, _ = vb.shape
    nb = t // BLK
    stat = pltpu.VMEM((2, BLK, LANES), F32)
    return pl.pallas_call(
        _attn_kernel,
        grid=(bsz, D_ATT // LANES, nb),
        in_specs=[pl.BlockSpec((1, 2, BLK, LANES), lambda b, hp, i: (b, hp, i, 0)),
                  pl.BlockSpec((1, nb, LANES, BLK), lambda b, hp, i: (b, 0, hp, 0)),
                  pl.BlockSpec((1, t, LANES), lambda b, hp, i: (b, 0, hp)),
                  pl.BlockSpec((2, 2, BLK, BLK), lambda b, hp, i: (hp, 0, 0, 0))],
        out_specs=pl.BlockSpec((1, BLK, LANES), lambda b, hp, i: (b, i, hp)),
        out_shape=jax.ShapeDtypeStruct((bsz, t, D_ATT), BF16),
        scratch_shapes=[stat, stat, stat, pltpu.VMEM((2, nb + 1, BLK, BLK), F32)],
        compiler_params=_cparams(("arbitrary", "arbitrary", "arbitrary")),
        name="moba_attention",
    )(qa, kt, vb, btab)


def _outproj_kernel(x_ref, yab_ref, yc_ref, w_ref, g_ref, b_ref, wrh_ref, wrl_ref, br_ref, x1_ref, comb_ref):
    mix = _dot(yab_ref[...], w_ref[0:512, :]) + _dot(yc_ref[...], w_ref[512:1024, :])
    x1 = _ln(ALPHA * x_ref[...] + mix, g_ref[...], b_ref[...])
    x1_ref[...] = x1
    x_hi = x1.astype(BF16)
    x_lo = (x1 - x_hi.astype(F32)).astype(BF16)
    lg = (_dot(x_hi, wrh_ref[...]) + (_dot(x_lo, wrh_ref[...]) + _dot(x_hi, wrl_ref[...]))) + br_ref[...]
    lanef = lax.broadcasted_iota(jnp.int32, lg.shape, 1).astype(F32)
    isg = (lanef >= N_EXPERTS) & (lanef < N_EXPERTS + N_GROUPS)
    gl = jnp.where(isg, lg, LOWEST)
    gexp = jnp.where(isg, jnp.exp(gl - jnp.max(gl, axis=1, keepdims=True)), 0.0)
    gprob = gexp / jnp.sum(gexp, axis=1, keepdims=True)
    g_w = jnp.max(gprob, axis=1, keepdims=True)
    g_idx = jnp.min(jnp.where(isg & (gprob == g_w), lanef, 1e9), axis=1, keepdims=True) - N_EXPERTS
    ise = (lanef >= EPG * g_idx) & (lanef < EPG * g_idx + EPG)
    el = jnp.where(ise, lg, LOWEST)
    eexp = jnp.where(ise, jnp.exp(el - jnp.max(el, axis=1, keepdims=True)), 0.0)
    eprob = eexp / jnp.sum(eexp, axis=1, keepdims=True)
    p1 = jnp.max(eprob, axis=1, keepdims=True)
    i1 = jnp.min(jnp.where(ise & (eprob == p1), lanef, 1e9), axis=1, keepdims=True)
    hit1 = lanef == i1
    rest = jnp.where(ise, jnp.where(hit1, -1.0, eprob), -1.0)
    p2 = jnp.max(rest, axis=1, keepdims=True)
    i2 = jnp.min(jnp.where(rest == p2, jnp.where(ise, jnp.where(hit1, 1e9, lanef), 1e9), 1e9),
                 axis=1, keepdims=True)
    hit2 = lanef == i2
    den = p1 + p2
    comb_ref[...] = jnp.where(hit1, g_w * (p1 / den), jnp.where(hit2, g_w * (p2 / den), 0.0))


def _outproj(x, yab, yc, w_out, g, b, wr, br, tm):
    n = x.shape[0]
    full = lambda i: (0, 0)
    wr_hi = wr.astype(BF16)
    wr_lo = (wr - wr_hi.astype(F32)).astype(BF16)
    return pl.pallas_call(
        _outproj_kernel,
        grid=(n // tm,),
        in_specs=[pl.BlockSpec((tm, D_MODEL), lambda i: (i, 0)),
                  pl.BlockSpec((tm, 512), lambda i: (i, 0)),
                  pl.BlockSpec((tm, 512), lambda i: (i, 0)),
                  pl.BlockSpec((D_MODEL, D_MODEL), full),
                  pl.BlockSpec((1, D_MODEL), full), pl.BlockSpec((1, D_MODEL), full),
                  pl.BlockSpec((D_MODEL, 128), full), pl.BlockSpec((D_MODEL, 128), full),
                  pl.BlockSpec((1, 128), full)],
        out_specs=[pl.BlockSpec((tm, D_MODEL), lambda i: (i, 0)),
                   pl.BlockSpec((tm, 128), lambda i: (i, 0))],
        out_shape=[jax.ShapeDtypeStruct((n, D_MODEL), F32), jax.ShapeDtypeStruct((n, 128), F32)],
        compiler_params=_cparams(("arbitrary",)),
        name="outproj_ln_route",
    )(x, yab, yc, w_out, g, b, wr_hi, wr_lo, br)


def _moe_kernel(x_ref, comb_ref, wgu_ref, wd_ref, g_ref, b_ref, o_ref, xb_ref, acc_ref):
    e = pl.program_id(1)

    @pl.when(e == 0)
    def _():
        xb_ref[...] = x_ref[...].astype(BF16)
        acc_ref[...] = jnp.zeros(acc_ref.shape, F32)

    gu = _dot(xb_ref[...], wgu_ref[0])
    hid = _silu(gu[:, 0:D_EXPERT]) * gu[:, D_EXPERT:2 * D_EXPERT]
    comb = comb_ref[...]
    lane = lax.broadcasted_iota(jnp.int32, comb.shape, 1)
    c = jnp.sum(jnp.where(lane == e, comb, 0.0), axis=1, keepdims=True)
    acc_ref[...] += _dot((hid * c).astype(BF16), wd_ref[0])

    @pl.when(e == pl.num_programs(1) - 1)
    def _():
        o_ref[...] = _ln(ALPHA * x_ref[...] + acc_ref[...], g_ref[...], b_ref[...])


def _moe(x1, comb, wgu, wd, g, b, tm):
    n = x1.shape[0]
    return pl.pallas_call(
        _moe_kernel,
        grid=(n // tm, N_EXPERTS),
        in_specs=[pl.BlockSpec((tm, D_MODEL), lambda i, e: (i, 0)),
                  pl.BlockSpec((tm, 128), lambda i, e: (i, 0)),
                  pl.BlockSpec((1, D_MODEL, 2 * D_EXPERT), lambda i, e: (e, 0, 0)),
                  pl.BlockSpec((1, D_EXPERT, D_MODEL), lambda i, e: (e, 0, 0)),
                  pl.BlockSpec((1, D_MODEL), lambda i, e: (0, 0)),
                  pl.BlockSpec((1, D_MODEL), lambda i, e: (0, 0))],
        out_specs=pl.BlockSpec((tm, D_MODEL), lambda i, e: (i, 0)),
        out_shape=jax.ShapeDtypeStruct((n, D_MODEL), F32),
        scratch_shapes=[pltpu.VMEM((tm, D_MODEL), BF16), pltpu.VMEM((tm, D_MODEL), F32)],
        compiler_params=_cparams(("arbitrary", "arbitrary")),
        name="moe_ffn_ln",
    )(x1, comb, wgu, wd, g, b)


def _sinproj_kernel(*refs, apply_ln):
    if apply_ln:
        (x_ref, lng_ref, lnb_ref, w_ref, caw_ref, cab_ref, calg_ref, calb_ref, cbw_ref, sta_ref, stb_ref,
         h_ref, q_ref, k_ref, v_ref, yab_ref, u_ref, ch_ref) = refs
    else:
        (x_ref, w_ref, caw_ref, cab_ref, calg_ref, calb_ref, cbw_ref, sta_ref, stb_ref,
         q_ref, k_ref, v_ref, yab_ref, u_ref, ch_ref) = refs
    x = x_ref[...]
    if apply_ln:
        x = _ln(x, lng_ref[...], lnb_ref[...])
        h_ref[...] = x
    xb = x.astype(BF16)

    def seg(a, b):
        return _dot(xb, w_ref[:, a:b])

    u = seg(0, 256) * jax.nn.sigmoid(seg(256, 512))
    u_ref[...] = u
    acc = caw_ref[CONF_WIDTH - 1:CONF_WIDTH, :] * u
    for j in range(CONF_WIDTH - 1):
        acc = acc + caw_ref[j:j + 1, :] * sta_ref[0, j]
    y_a = _silu(_ln(acc + cab_ref[...], calg_ref[...], calb_ref[...]))
    yab_ref[:, 0:256] = y_a.astype(BF16)

    ch = seg(768, 1024) * seg(1024, 1280)
    ch_ref[...] = ch
    accb = cbw_ref[SCONV_WIDTH - 1:SCONV_WIDTH, :] * ch
    for j in range(SCONV_WIDTH - 1):
        accb = accb + cbw_ref[j:j + 1, :] * stb_ref[0, j]
    yab_ref[:, 256:512] = (seg(512, 768) * accb).astype(BF16)

    q_ref[...] = seg(1280, 1792) * ATT_SCALE
    k_ref[...] = seg(1792, 2304)
    v_ref[...] = seg(2304, 2816)


def _sinproj(x, ln_g, ln_b, w_in, caw, cab, calg, calb, cbw, sta_t, stb_t, layer, apply_ln):
    n = x.shape[0]
    full = lambda i: (0, 0)
    in_specs = [pl.BlockSpec((n, D_MODEL), full)]
    args = [x]
    if apply_ln:
        in_specs += [pl.BlockSpec((1, D_MODEL), full), pl.BlockSpec((1, D_MODEL), full)]
        args += [ln_g, ln_b]
    in_specs += [pl.BlockSpec((D_MODEL, D_IN), full),
                 pl.BlockSpec((CONF_WIDTH, D_CONF), full),
                 pl.BlockSpec((1, D_CONF), full), pl.BlockSpec((1, D_CONF), full), pl.BlockSpec((1, D_CONF), full),
                 pl.BlockSpec((SCONV_WIDTH, D_SCONV), full),
                 pl.BlockSpec((1, CONF_WIDTH - 1, n, D_CONF), lambda i: (layer, 0, 0, 0)),
                 pl.BlockSpec((1, SCONV_WIDTH - 1, n, D_SCONV), lambda i: (layer, 0, 0, 0))]
    args += [w_in, caw, cab, calg, calb, cbw, sta_t, stb_t]
    out_specs, out_shape = [], []
    if apply_ln:
        out_specs.append(pl.BlockSpec((n, D_MODEL), full))
        out_shape.append(jax.ShapeDtypeStruct((n, D_MODEL), F32))
    out_specs += [pl.BlockSpec((n, D_ATT), full)] * 3 + [pl.BlockSpec((n, 512), full),
                                                        pl.BlockSpec((n, D_CONF), full),
                                                        pl.BlockSpec((n, D_SCONV), full)]
    out_shape += [jax.ShapeDtypeStruct((n, D_ATT), F32)] * 3 + [jax.ShapeDtypeStruct((n, 512), BF16),
                                                               jax.ShapeDtypeStruct((n, D_CONF), F32),
                                                               jax.ShapeDtypeStruct((n, D_SCONV), F32)]
    return pl.pallas_call(
        functools.partial(_sinproj_kernel, apply_ln=apply_ln),
        grid=(1,),
        in_specs=in_specs, out_specs=out_specs, out_shape=out_shape,
        compiler_params=_cparams(("arbitrary",)),
        name="sample_inproj_conv",
    )(*args)


N_PAGES = 16
N_PAST_BLK = N_PAGES * PAGE // BLK
SEL_STRIDE = 16


def _page_scores(kt_page, qcol):
    return jnp.sum(kt_page * qcol, axis=0, keepdims=True)


def _sgate_kernel(pt_ref, qt_ref, ptv_ref, *refs):
    krefs = refs[:N_PAGES]
    out_ref, sc_ref = refs[N_PAGES], refs[N_PAGES + 1]
    lane1 = lax.broadcasted_iota(jnp.int32, (1, LANES), 1)
    rows = []
    for hd in range(N_HEADS):
        qcol = jnp.broadcast_to(qt_ref[0, :, hd:hd + 1], (HEAD_DIM, PAGE))
        gh = jnp.zeros((1, LANES), F32)
        for n in range(N_PAST_BLK):
            sc0 = _page_scores(krefs[2 * n][0, 0, hd], qcol)
            sc1 = _page_scores(krefs[2 * n + 1][0, 0, hd], qcol)
            sc_ref[0, hd, 2 * n:2 * n + 1, :] = sc0
            sc_ref[0, hd, 2 * n + 1:2 * n + 2, :] = sc1
            gh = jnp.where(lane1 == n, jnp.sum(sc0 + sc1, axis=1, keepdims=True), gh)
        rows.append(gh)
    g = jnp.concatenate(rows, axis=0)
    lane = lax.broadcasted_iota(jnp.int32, (N_HEADS, LANES), 1)
    lanef = lane.astype(F32)
    g = jnp.where(lane < N_PAST_BLK, g, LOWEST)
    ptf = ptv_ref[0].astype(F32)
    l16 = lax.broadcasted_iota(jnp.int32, (N_HEADS, N_PAGES), 1).astype(F32)
    res = jnp.zeros((N_HEADS, LANES), F32)
    for r in range(TOPK):
        mx = jnp.max(g, axis=1, keepdims=True)
        idx = jnp.min(jnp.where(g == mx, lanef, 1e9), axis=1, keepdims=True)
        g = jnp.where(lanef == idx, LOWEST, g)
        pg0 = jnp.sum(jnp.where(l16 == 2.0 * idx, ptf, 0.0), axis=1, keepdims=True)
        pg1 = jnp.sum(jnp.where(l16 == 2.0 * idx + 1.0, ptf, 0.0), axis=1, keepdims=True)
        res = jnp.where(lane == 2 * r, pg0, res)
        res = jnp.where(lane == 2 * r + 1, pg1, res)
        res = jnp.where(lane == 8 + r, idx, res)
    out_ref[0] = res.astype(jnp.int32)


def _sgate(page_table, qt8, cache_kt, layer):
    nseq = page_table.shape[0]
    ptv = page_table.reshape(nseq, 1, N_PAGES)

    def kspec(j):
        return pl.BlockSpec((1, 1, N_HEADS, HEAD_DIM, PAGE), lambda b, pt: (layer, pt[b * N_PAGES + j], 0, 0, 0))

    grid_spec = pltpu.PrefetchScalarGridSpec(
        num_scalar_prefetch=1,
        grid=(nseq,),
        in_specs=[pl.BlockSpec((1, HEAD_DIM, N_HEADS), lambda b, pt: (b, 0, 0)),
                  pl.BlockSpec((1, 1, N_PAGES), lambda b, pt: (b, 0, 0))]
                 + [kspec(j) for j in range(N_PAGES)],
        out_specs=[pl.BlockSpec((1, N_HEADS, LANES), lambda b, pt: (b, 0, 0)),
                   pl.BlockSpec((1, N_HEADS, N_PAGES, PAGE), lambda b, pt: (b, 0, 0, 0))],
    )
    return pl.pallas_call(
        _sgate_kernel,
        grid_spec=grid_spec,
        out_shape=[jax.ShapeDtypeStruct((nseq, N_HEADS, LANES), jnp.int32),
                   jax.ShapeDtypeStruct((nseq, N_HEADS, N_PAGES, PAGE), F32)],
        compiler_params=_cparams(("arbitrary",)),
        name="sample_block_gate",
    )(page_table.reshape(-1), qt8, ptv, *([cache_kt] * N_PAGES))


N_SEL_PAGES = TOPK * BLK // PAGE


def _sattn_kernel(sel_ref, rb_ref, q_ref, kn_ref, vnt_ref, stab_ref, sc_ref, *refs):
    np_ = N_HEADS * N_SEL_PAGES
    vrefs = refs[:np_]
    o_ref = refs[np_]
    b = pl.program_id(0)
    s_self_all = jnp.sum(q_ref[0] * kn_ref[0], axis=1, keepdims=True)
    for hd in range(N_HEADS):
        base = (b * N_HEADS + hd) * SEL_STRIDE
        c31 = rb_ref[N_BUCKETS - 1, hd]
        scores = []
        for s in range(N_SEL_PAGES):
            blk = sel_ref[base + 8 + s // 2]
            sc = sc_ref[0, hd, pl.ds(2 * blk + s % 2, 1), :]
            tab = stab_ref[hd, (s % 2):(s % 2) + 1, :]
            scores.append(sc + jnp.where(blk == N_PAST_BLK - 1, tab, c31))
        s_self = s_self_all[hd:hd + 1, :] + rb_ref[0, hd]
        m = s_self
        for sc in scores:
            m = jnp.maximum(m, jnp.max(sc, axis=1, keepdims=True))
        e_self = jnp.exp(s_self - m)
        den = e_self
        acc = jnp.zeros((HEAD_DIM, PAGE), F32)
        for s in range(N_SEL_PAGES):
            e = jnp.exp(scores[s] - m)
            den = den + jnp.sum(e, axis=1, keepdims=True)
            acc = acc + vrefs[hd * N_SEL_PAGES + s][0, 0, 0] * e
        num = jnp.sum(acc, axis=1, keepdims=True) + e_self * vnt_ref[0, :, hd:hd + 1]
        o_ref[0, :, hd:hd + 1] = num / den


def _sattn(sel, rel_bias, q8, kn8, vnt8, stab, scores, cache_vt, layer):
    nseq = q8.shape[0]

    def pspec(hd, s):
        return pl.BlockSpec((1, 1, 1, HEAD_DIM, PAGE),
                            lambda b, sl: (layer, sl[(b * N_HEADS + hd) * SEL_STRIDE + s], hd, 0, 0))

    def page_specs():
        return [pspec(hd, s) for hd in range(N_HEADS) for s in range(N_SEL_PAGES)]

    def row():
        return pl.BlockSpec((1, N_HEADS, HEAD_DIM), lambda b, sl: (b, 0, 0))

    def col():
        return pl.BlockSpec((1, HEAD_DIM, N_HEADS), lambda b, sl: (b, 0, 0))

    grid_spec = pltpu.PrefetchScalarGridSpec(
        num_scalar_prefetch=1,
        grid=(nseq,),
        in_specs=[pl.BlockSpec(memory_space=pltpu.SMEM), row(), row(), col(),
                  pl.BlockSpec((N_HEADS, 8, PAGE), lambda b, sl: (0, 0, 0)),
                  pl.BlockSpec((1, N_HEADS, N_PAGES, PAGE), lambda b, sl: (b, 0, 0, 0))] + page_specs(),
        out_specs=col(),
    )
    npg = N_HEADS * N_SEL_PAGES
    return pl.pallas_call(
        _sattn_kernel,
        grid_spec=grid_spec,
        out_shape=jax.ShapeDtypeStruct((nseq, HEAD_DIM, N_HEADS), F32),
        compiler_params=_cparams(("arbitrary",)),
        name="sample_attention",
    )(sel.reshape(-1), rel_bias, q8, kn8, vnt8, stab, scores, *([cache_vt] * npg))


TM_OUT = 512
TM_MOE = 1024


def kernel(x_prompt, x_sample, cache_k, cache_v, state_conv_a, state_conv_b, page_table, ln_in_g, ln_in_b, w_in, conv_a_w, conv_a_b, conv_a_ln_g, conv_a_ln_b, conv_b_w, rel_bias, w_out, ln1_g, ln1_b, w_group, b_group, w_router, b_router, w_gate, w_up, w_down, ln2_g, ln2_b):
    bsz, t, _ = x_prompt.shape
    nseq = x_sample.shape[0]
    n_p = bsz * t
    row = lambda a: a.reshape(1, -1)

    btab, stab = _bias_tables(rel_bias)
    sta_t = state_conv_a.transpose(0, 2, 1, 3)
    stb_t = state_conv_b.transpose(0, 2, 1, 3)
    cache_kt = cache_k.transpose(0, 1, 2, 4, 3)
    cache_vt = cache_v.transpose(0, 1, 2, 4, 3)
    x_s = x_sample.reshape(nseq, D_MODEL)
    kpages = jnp.zeros((DEPTH, bsz, t // PAGE, N_HEADS, HEAD_DIM, PAGE), F32)
    vpages = jnp.zeros((DEPTH, bsz, t // PAGE, N_HEADS, HEAD_DIM, PAGE), F32)

    h_p = x_prompt
    h_s = x_s
    ap_l, bp_l, ks_l, vs_l, us_l, cs_l = [], [], [], [], [], []
    for l in range(DEPTH):
        first = l == 0
        w_in_b = w_in[l].astype(BF16)
        w_out_b = w_out[l].astype(BF16)
        wgu = jnp.concatenate([w_gate[l], w_up[l]], axis=-1).astype(BF16)
        wd = w_down[l].astype(BF16)
        wr = jnp.pad(jnp.concatenate([w_router[l], w_group[l]], axis=-1), ((0, 0), (0, 128 - N_EXPERTS - N_GROUPS)))
        br = jnp.pad(jnp.concatenate([b_router[l], b_group[l]]), (0, 128 - N_EXPERTS - N_GROUPS)).reshape(1, 128)
        conv_args = (conv_a_w[l], row(conv_a_b[l]), row(conv_a_ln_g[l]), row(conv_a_ln_b[l]), conv_b_w[l])

        outs = _inproj(h_p, row(ln_in_g), row(ln_in_b), w_in_b, *conv_args, kpages, vpages, l, apply_ln=first)
        if first:
            h_p, outs = outs[0], outs[1:]
        qa, kt, vb, kpages, vpages, yab, new_a, new_b = outs
        yc = _attention(qa, kt, vb, btab)
        x1, comb = _outproj(h_p.reshape(n_p, D_MODEL), yab.reshape(n_p, 512), yc.reshape(n_p, D_ATT),
                            w_out_b, row(ln1_g[l]), row(ln1_b[l]), wr, br, TM_OUT)
        h_p = _moe(x1, comb, wgu, wd, row(ln2_g[l]), row(ln2_b[l]), TM_MOE).reshape(bsz, t, D_MODEL)
        ap_l.append(new_a)
        bp_l.append(new_b)

        souts = _sinproj(h_s, row(ln_in_g), row(ln_in_b), w_in_b, *conv_args, sta_t, stb_t, l, apply_ln=first)
        if first:
            h_s, souts = souts[0], souts[1:]
        q_s, k_s, v_s, yab_s, u_s, ch_s = souts
        q8 = q_s.reshape(nseq, N_HEADS, HEAD_DIM)
        k8 = k_s.reshape(nseq, N_HEADS, HEAD_DIM)
        v8 = v_s.reshape(nseq, N_HEADS, HEAD_DIM)
        qt8 = q8.transpose(0, 2, 1)
        sel, scores_s = _sgate(page_table, qt8, cache_kt, l)
        yc_s = _sattn(sel[:, :, :SEL_STRIDE], rel_bias, q8, k8, v8.transpose(0, 2, 1), stab, scores_s, cache_vt, l)
        yc_s = yc_s.transpose(0, 2, 1).reshape(nseq, D_ATT).astype(BF16)
        x1_s, comb_s = _outproj(h_s, yab_s, yc_s, w_out_b, row(ln1_g[l]), row(ln1_b[l]), wr, br, nseq)
        h_s = _moe(x1_s, comb_s, wgu, wd, row(ln2_g[l]), row(ln2_b[l]), nseq)
        ks_l.append(k8.reshape(nseq, N_HEADS, 1, HEAD_DIM))
        vs_l.append(v8.reshape(nseq, N_HEADS, 1, HEAD_DIM))
        us_l.append(u_s)
        cs_l.append(ch_s)

    new_a_s = jnp.concatenate([state_conv_a[:, :, 1:], jnp.stack(us_l)[:, :, None, :]], axis=2)
    new_b_s = jnp.concatenate([state_conv_b[:, :, 1:], jnp.stack(cs_l)[:, :, None, :]], axis=2)
    return (h_p, h_s.reshape(nseq, 1, D_MODEL),
            kpages.transpose(0, 1, 2, 3, 5, 4), vpages.transpose(0, 1, 2, 3, 5, 4),
            jnp.stack(ap_l), jnp.stack(bp_l), jnp.stack(ks_l), jnp.stack(vs_l), new_a_s, new_b_s)
```

```python
import functools
import math

import numpy as np
import jax
import jax.numpy as jnp
from jax import lax
from jax.experimental import pallas as pl
from jax.experimental.pallas import tpu as pltpu

D_MODEL = 1024
HEAD_DIM = 64
D_CONF = 256
D_SCONV = 256
D_ATT = 512
N_HEADS = 8
CONF_WIDTH = 31
SCONV_WIDTH = 3
PAGE = 128
BLK = 256
TOPK = 3
N_BUCKETS = 32
MAX_DISTANCE = 128
N_GROUPS = 4
EPG = 4
N_EXPERTS = 16
D_EXPERT = 256
DEPTH = 4
ALPHA = (2 * DEPTH) ** 0.25
LN_EPS = 1e-5
NEG = -1e30
LOWEST = -3e38
ATT_SCALE = HEAD_DIM ** -0.5
LOG2E = math.log2(math.e)
D_IN = 2 * D_CONF + 3 * D_SCONV + 3 * D_ATT
LANES = 128
PEN_LANES = 16

F32 = jnp.float32
BF16 = jnp.bfloat16
HIGHEST = lax.Precision.HIGHEST

VMEM_LIMIT = 56 * 1024 * 1024


def _cparams(sem):
    return pltpu.CompilerParams(dimension_semantics=sem, vmem_limit_bytes=VMEM_LIMIT)


def _ln(x, g, b):
    mu = jnp.mean(x, axis=-1, keepdims=True)
    xc = x - mu
    var = jnp.mean(xc * xc, axis=-1, keepdims=True)
    return xc * lax.rsqrt(var + LN_EPS) * g + b


def _silu(x):
    return x * jax.nn.sigmoid(x)


def _dot(a, b):
    return jnp.dot(a, b, preferred_element_type=F32)


def _gate_row(head, n):
    return HEAD_DIM * (1 - head % 2) + PEN_LANES * (head // 2) + n


def _top3_rows(gate, rowf, valid, axis=0):
    g = jnp.where(valid, gate, NEG)
    sel = jnp.zeros(gate.shape, F32)
    for _ in range(TOPK):
        mx = jnp.max(g, axis=axis, keepdims=True)
        idx = jnp.min(jnp.where(g == mx, rowf, 1e9), axis=axis, keepdims=True)
        hit = rowf == idx
        sel = jnp.where(hit, 1.0, sel)
        g = jnp.where(hit, LOWEST, g)
    return jnp.where(valid, sel, 0.0)


def _t5_bucket_np(dist):
    n = np.maximum(dist, 0)
    max_exact = N_BUCKETS // 2
    nf = np.maximum(n, 1).astype(np.float32)
    large = max_exact + (np.log(nf / np.float32(max_exact)) / np.float32(math.log(MAX_DISTANCE / max_exact))
                         * np.float32(N_BUCKETS - max_exact)).astype(np.int32)
    large = np.minimum(large, N_BUCKETS - 1)
    return np.where(n < max_exact, n, large).astype(np.int32)


def _bias_kernel(rb_ref, idx0_ref, idx1_ref, idxs_ref, btab_ref, stab_ref):
    h = pl.program_id(0)
    idx0 = idx0_ref[...]
    idx1 = idx1_ref[...]
    idxs = idxs_ref[...]
    t0 = jnp.zeros(idx0.shape, F32)
    t1 = jnp.zeros(idx1.shape, F32)
    ts = jnp.zeros(idxs.shape, F32)
    for bkt in range(N_BUCKETS):
        val = rb_ref[bkt, h]
        t0 = jnp.where(idx0 == bkt, val, t0)
        t1 = jnp.where(idx1 == bkt, val, t1)
        ts = jnp.where(idxs == bkt, val, ts)
    far = rb_ref[N_BUCKETS - 1, h]
    btab_ref[0, 0] = (t0 - far) * LOG2E
    btab_ref[0, 1] = (t1 - far) * LOG2E
    stab_ref[0] = ts


def _bias_tables(rel_bias):
    r = np.arange(BLK)[:, None]
    c = np.arange(BLK)[None, :]
    idx0 = _t5_bucket_np(r - c)
    idx1 = _t5_bucket_np(BLK + r - c)
    idxs = np.zeros((8, PAGE), np.int32)
    idxs[0:2] = _t5_bucket_np(BLK - np.arange(BLK)).reshape(2, PAGE)
    return pl.pallas_call(
        _bias_kernel,
        grid=(N_HEADS,),
        in_specs=[pl.BlockSpec(memory_space=pltpu.SMEM),
                  pl.BlockSpec((BLK, BLK), lambda h: (0, 0)),
                  pl.BlockSpec((BLK, BLK), lambda h: (0, 0)),
                  pl.BlockSpec((8, PAGE), lambda h: (0, 0))],
        out_specs=[pl.BlockSpec((1, 2, BLK, BLK), lambda h: (h, 0, 0, 0)),
                   pl.BlockSpec((1, 8, PAGE), lambda h: (h, 0, 0))],
        out_shape=[jax.ShapeDtypeStruct((N_HEADS, 2, BLK, BLK), F32),
                   jax.ShapeDtypeStruct((N_HEADS, 8, PAGE), F32)],
        compiler_params=_cparams(("arbitrary",)),
        name="bias_tables",
    )(rel_bias, jnp.asarray(idx0), jnp.asarray(idx1), jnp.asarray(idxs))


TM1 = BLK


def _inproj_kernel(*refs, apply_ln):
    if apply_ln:
        (x_ref, lng_ref, lnb_ref, w_ref, caw_ref, cab_ref, calg_ref, calb_ref, cbw_ref, _kin, _vin,
         h_ref, qa_ref, kt_ref, vb_ref, kpg_ref, vpg_ref, yab_ref, newa_ref, newb_ref,
         ua_ref, ub_ref, ksh_ref) = refs
    else:
        (x_ref, w_ref, caw_ref, cab_ref, calg_ref, calb_ref, cbw_ref, _kin, _vin,
         qa_ref, kt_ref, vb_ref, kpg_ref, vpg_ref, yab_ref, newa_ref, newb_ref,
         ua_ref, ub_ref, ksh_ref) = refs
    ti = pl.program_id(1)
    last = pl.num_programs(1) - 1
    x = x_ref[0]
    if apply_ln:
        x = _ln(x, lng_ref[...], lnb_ref[...])
        h_ref[0] = x
    xb = x.astype(BF16)

    def seg(a, b):
        return _dot(xb, w_ref[:, a:b])

    @pl.when(ti == 0)
    def _():
        ua_ref[0:32, :] = jnp.zeros((32, D_CONF), F32)
        ub_ref[0:8, :] = jnp.zeros((8, D_SCONV), F32)
        ksh_ref[...] = jnp.zeros(ksh_ref.shape, F32)

    u = seg(0, 256) * jax.nn.sigmoid(seg(256, 512))
    ua_ref[32:32 + TM1, :] = u
    acc = jnp.zeros((TM1, D_CONF), F32)
    for j in range(CONF_WIDTH):
        acc = acc + caw_ref[j:j + 1, :] * ua_ref[2 + j:2 + j + TM1, :]
    y_a = _silu(_ln(acc + cab_ref[...], calg_ref[...], calb_ref[...]))
    yab_ref[0, :, 0:256] = y_a.astype(BF16)

    ch = seg(768, 1024) * seg(1024, 1280)
    ub_ref[8:8 + TM1, :] = ch
    accb = jnp.zeros((TM1, D_SCONV), F32)
    for j in range(SCONV_WIDTH):
        accb = accb + cbw_ref[j:j + 1, :] * ub_ref[6 + j:6 + j + TM1, :]
    yab_ref[0, :, 256:512] = (seg(512, 768) * accb).astype(BF16)

    @pl.when(ti == last)
    def _():
        newa_ref[0] = ua_ref[TM1 + 2:TM1 + 32, :]
        newb_ref[0] = ub_ref[TM1 + 6:TM1 + 8, :]

    ua_ref[0:32, :] = ua_ref[TM1:TM1 + 32, :]
    ub_ref[0:8, :] = ub_ref[TM1:TM1 + 8, :]

    q = seg(1280, 1792) * (ATT_SCALE * LOG2E)
    k = seg(1792, 2304)
    v = seg(2304, 2816)

    ks = ksh_ref[...]
    ks_hi = ks.astype(BF16)
    ks_lo = (ks - ks_hi.astype(F32)).astype(BF16)
    q_hi = q.astype(BF16)
    q_lo = (q - q_hi.astype(F32)).astype(BF16)
    nt_dims = (((1,), (1,)), ((), ()))
    gate = (lax.dot_general(ks_hi, q_hi, nt_dims, preferred_element_type=F32)
            + (lax.dot_general(ks_lo, q_hi, nt_dims, preferred_element_type=F32)
               + lax.dot_general(ks_hi, q_lo, nt_dims, preferred_element_type=F32)))
    g3 = gate.reshape(N_HEADS, PEN_LANES, TM1)
    rown3 = lax.broadcasted_iota(jnp.int32, g3.shape, 1)
    sel3 = _top3_rows(g3, rown3.astype(F32), rown3 < ti, axis=1)
    pen_all = jnp.where(sel3 > 0.5, 0.0, NEG).reshape(N_HEADS * PEN_LANES, TM1).T
    lane = lax.broadcasted_iota(jnp.int32, (TM1, LANES), 1)
    for hd in range(N_HEADS):
        hh, pair = hd % 2, hd // 2
        in_q = (lane >= HEAD_DIM * hh) & (lane < HEAD_DIM * (hh + 1))
        qa_ref[0, hd] = jnp.where(in_q, q[:, pair * LANES:(pair + 1) * LANES], pen_all).astype(BF16)
    ksum = jnp.sum(k, axis=0, keepdims=True)
    col = lax.broadcasted_iota(jnp.int32, (1, D_ATT), 1)
    for hd in range(N_HEADS):
        in_h = (col >= HEAD_DIM * hd) & (col < HEAD_DIM * (hd + 1))
        ksh_ref[pl.ds(_gate_row(hd, ti), 1), :] = jnp.where(in_h, ksum, 0.0)

    kt = k.T
    vt = v.T
    kt_ref[0, 0] = kt.astype(BF16)
    vb_ref[0] = v.astype(BF16)
    for p in range(TM1 // PAGE):
        for hd in range(N_HEADS):
            kpg_ref[0, 0, p, hd] = kt[hd * HEAD_DIM:(hd + 1) * HEAD_DIM, p * PAGE:(p + 1) * PAGE]
            vpg_ref[0, 0, p, hd] = vt[hd * HEAD_DIM:(hd + 1) * HEAD_DIM, p * PAGE:(p + 1) * PAGE]


def _inproj(x, ln_g, ln_b, w_in, caw, cab, calg, calb, cbw, kpages, vpages, layer, apply_ln):
    bsz, t, _ = x.shape
    nt = t // TM1
    full2 = lambda b, i: (0, 0)
    in_specs = [pl.BlockSpec((1, TM1, D_MODEL), lambda b, i: (b, i, 0))]
    args = [x]
    if apply_ln:
        in_specs += [pl.BlockSpec((1, D_MODEL), full2), pl.BlockSpec((1, D_MODEL), full2)]
        args += [ln_g, ln_b]
    in_specs += [pl.BlockSpec((D_MODEL, D_IN), full2),
                 pl.BlockSpec((CONF_WIDTH, D_CONF), full2),
                 pl.BlockSpec((1, D_CONF), full2), pl.BlockSpec((1, D_CONF), full2),
                 pl.BlockSpec((1, D_CONF), full2),
                 pl.BlockSpec((SCONV_WIDTH, D_SCONV), full2),
                 pl.BlockSpec(memory_space=pl.ANY), pl.BlockSpec(memory_space=pl.ANY)]
    args += [w_in, caw, cab, calg, calb, cbw, kpages, vpages]
    kpages_in = len(args) - 2
    n_extra = 1 if apply_ln else 0
    page_block = (1, 1, TM1 // PAGE, N_HEADS, HEAD_DIM, PAGE)
    page_map = lambda b, i: (layer, b, i, 0, 0, 0)
    out_specs, out_shape = [], []
    if apply_ln:
        out_specs.append(pl.BlockSpec((1, TM1, D_MODEL), lambda b, i: (b, i, 0)))
        out_shape.append(jax.ShapeDtypeStruct((bsz, t, D_MODEL), F32))
    out_specs += [
        pl.BlockSpec((1, N_HEADS, TM1, LANES), lambda b, i: (b, 0, i, 0)),
        pl.BlockSpec((1, 1, D_ATT, TM1), lambda b, i: (b, i, 0, 0)),
        pl.BlockSpec((1, TM1, D_ATT), lambda b, i: (b, i, 0)),
        pl.BlockSpec(page_block, page_map),
        pl.BlockSpec(page_block, page_map),
        pl.BlockSpec((1, TM1, 512), lambda b, i: (b, i, 0)),
        pl.BlockSpec((1, CONF_WIDTH - 1, D_CONF), lambda b, i: (b, 0, 0)),
        pl.BlockSpec((1, SCONV_WIDTH - 1, D_SCONV), lambda b, i: (b, 0, 0)),
    ]
    out_shape += [
        jax.ShapeDtypeStruct((bsz, N_HEADS, t, LANES), BF16),
        jax.ShapeDtypeStruct((bsz, nt, D_ATT, TM1), BF16),
        jax.ShapeDtypeStruct((bsz, t, D_ATT), BF16),
        jax.ShapeDtypeStruct(kpages.shape, F32),
        jax.ShapeDtypeStruct(vpages.shape, F32),
        jax.ShapeDtypeStruct((bsz, t, 512), BF16),
        jax.ShapeDtypeStruct((bsz, CONF_WIDTH - 1, D_CONF), F32),
        jax.ShapeDtypeStruct((bsz, SCONV_WIDTH - 1, D_SCONV), F32),
    ]
    assert nt <= PEN_LANES
    return pl.pallas_call(
        functools.partial(_inproj_kernel, apply_ln=apply_ln),
        grid=(bsz, nt),
        in_specs=in_specs, out_specs=out_specs, out_shape=out_shape,
        input_output_aliases={kpages_in: n_extra + 3, kpages_in + 1: n_extra + 4},
        scratch_shapes=[pltpu.VMEM((32 + TM1, D_CONF), F32), pltpu.VMEM((8 + TM1, D_SCONV), F32),
                        pltpu.VMEM((N_HEADS * PEN_LANES, D_ATT), F32)],
        compiler_params=_cparams(("arbitrary", "arbitrary")),
        name="inproj_conv",
    )(*args)


def _attn_kernel(qa_ref, kt_ref, v_ref, btab_ref, o_ref, m_ref, l_ref, acc_ref, s_ref):
    i = pl.program_id(2)
    nb = kt_ref.shape[1]
    lane = lax.broadcasted_iota(jnp.int32, (BLK, LANES), 1)
    rowd = lax.broadcasted_iota(jnp.int32, (LANES, BLK), 0)
    rq = lax.broadcasted_iota(jnp.int32, (BLK, BLK), 0)
    ck = lax.broadcasted_iota(jnp.int32, (BLK, BLK), 1)

    qa, head_rows, pen_base = [], [], []
    for hh in range(2):
        in_t = (rowd >= HEAD_DIM * hh) & (rowd < HEAD_DIM * (hh + 1))
        qa.append(qa_ref[0, hh])
        head_rows.append(jnp.where(in_t, 1.0, 0.0).astype(BF16))
        pen_base.append(HEAD_DIM * (1 - hh) + PEN_LANES * pl.program_id(1))
        m_ref[hh] = jnp.full((BLK, LANES), LOWEST, F32)
        l_ref[hh] = jnp.zeros((BLK, LANES), F32)
        acc_ref[hh] = jnp.zeros((BLK, LANES), F32)

    def scores(hh, ktn, n, kind):
        wk = ktn * head_rows[hh]
        if kind == "own":
            return jnp.where(ck <= rq, _dot(qa[hh], wk) + btab_ref[hh, 0], NEG)
        wk = wk + jnp.where(rowd == pen_base[hh] + n, 1.0, 0.0).astype(BF16)
        s = _dot(qa[hh], wk)
        if kind == "prev":
            s = s + btab_ref[hh, 1]
        return s

    def pass1(blocks):
        kts = [kt_ref[0, n] for n, _, _ in blocks]
        for hh in range(2):
            mx = m_ref[hh]
            for (n, kind, slot), ktn in zip(blocks, kts):
                s = scores(hh, ktn, n, kind)
                s_ref[hh, slot] = s
                mx = jnp.maximum(mx, jnp.maximum(s[:, :LANES], s[:, LANES:]))
            m_ref[hh] = mx

    def pass2(blocks):
        vs = [v_ref[0, pl.ds(pl.multiple_of(n * BLK, BLK), BLK), :] for n, _ in blocks]
        for hh in range(2):
            mb = m_ref[hh]
            lsum = l_ref[hh]
            osum = acc_ref[hh]
            for (n, slot), vn in zip(blocks, vs):
                p0 = jnp.exp2(s_ref[hh, slot, :, :LANES] - mb)
                p1 = jnp.exp2(s_ref[hh, slot, :, LANES:] - mb)
                lsum = lsum + (p0 + p1)
                osum = osum + _dot(jnp.concatenate([p0, p1], axis=1).astype(BF16), vn)
            l_ref[hh] = lsum
            acc_ref[hh] = osum

    n_far = jnp.maximum(i - 1, 0)
    n_prev = jnp.maximum(i - 1, 0)
    own_slot = nb
    n_pairs = n_far // 2
    odd = n_far - 2 * n_pairs

    def far1(j, c):
        pass1([(2 * j, "far", 2 * j), (2 * j + 1, "far", 2 * j + 1)])
        return c

    def far2(j, c):
        pass2([(2 * j, 2 * j), (2 * j + 1, 2 * j + 1)])
        return c

    lax.fori_loop(0, n_pairs, far1, 0)

    @pl.when(odd == 1)
    def _():
        pass1([(n_far - 1, "far", n_far - 1)])

    pass1([(n_prev, "prev", n_prev), (i, "own", own_slot)])
    for hh in range(2):
        m_ref[hh] = jnp.broadcast_to(jnp.max(m_ref[hh], axis=1, keepdims=True), (BLK, LANES))
    lax.fori_loop(0, n_pairs, far2, 0)

    @pl.when(odd == 1)
    def _():
        pass2([(n_far - 1, n_far - 1)])

    pass2([(n_prev, n_prev), (i, own_slot)])

    o0 = acc_ref[0] / jnp.sum(l_ref[0], axis=1, keepdims=True)
    o1 = acc_ref[1] / jnp.sum(l_ref[1], axis=1, keepdims=True)
    o_ref[0] = jnp.where(lane < HEAD_DIM, o0, o1).astype(BF16)


def _attention(qa, kt, vb, btab):
    bsz, t, _ = vb.shape
    nb = t // BLK
    stat = pltpu.VMEM((2, BLK, LANES), F32)
    return pl.pallas_call(
        _attn_kernel,
        grid=(bsz, D_ATT // LANES, nb),
        in_specs=[pl.BlockSpec((1, 2, BLK, LANES), lambda b, hp, i: (b, hp, i, 0)),
                  pl.BlockSpec((1, nb, LANES, BLK), lambda b, hp, i: (b, 0, hp, 0)),
                  pl.BlockSpec((1, t, LANES), lambda b, hp, i: (b, 0, hp)),
                  pl.BlockSpec((2, 2, BLK, BLK), lambda b, hp, i: (hp, 0, 0, 0))],
        out_specs=pl.BlockSpec((1, BLK, LANES), lambda b, hp, i: (b, i, hp)),
        out_shape=jax.ShapeDtypeStruct((bsz, t, D_ATT), BF16),
        scratch_shapes=[stat, stat, stat, pltpu.VMEM((2, nb + 1, BLK, BLK), F32)],
        compiler_params=_cparams(("arbitrary", "arbitrary", "arbitrary")),
        name="moba_attention",
    )(qa, kt, vb, btab)


def _outproj_kernel(x_ref, yab_ref, yc_ref, w_ref, g_ref, b_ref, wrh_ref, wrl_ref, br_ref, x1_ref, comb_ref):
    mix = _dot(yab_ref[...], w_ref[0:512, :]) + _dot(yc_ref[...], w_ref[512:1024, :])
    x1 = _ln(ALPHA * x_ref[...] + mix, g_ref[...], b_ref[...])
    x1_ref[...] = x1
    x_hi = x1.astype(BF16)
    x_lo = (x1 - x_hi.astype(F32)).astype(BF16)
    lg = (_dot(x_hi, wrh_ref[...]) + (_dot(x_lo, wrh_ref[...]) + _dot(x_hi, wrl_ref[...]))) + br_ref[...]
    lanef = lax.broadcasted_iota(jnp.int32, lg.shape, 1).astype(F32)
    isg = (lanef >= N_EXPERTS) & (lanef < N_EXPERTS + N_GROUPS)
    gl = jnp.where(isg, lg, LOWEST)
    gexp = jnp.where(isg, jnp.exp(gl - jnp.max(gl, axis=1, keepdims=True)), 0.0)
    gprob = gexp / jnp.sum(gexp, axis=1, keepdims=True)
    g_w = jnp.max(gprob, axis=1, keepdims=True)
    g_idx = jnp.min(jnp.where(isg & (gprob == g_w), lanef, 1e9), axis=1, keepdims=True) - N_EXPERTS
    ise = (lanef >= EPG * g_idx) & (lanef < EPG * g_idx + EPG)
    el = jnp.where(ise, lg, LOWEST)
    eexp = jnp.where(ise, jnp.exp(el - jnp.max(el, axis=1, keepdims=True)), 0.0)
    eprob = eexp / jnp.sum(eexp, axis=1, keepdims=True)
    p1 = jnp.max(eprob, axis=1, keepdims=True)
    i1 = jnp.min(jnp.where(ise & (eprob == p1), lanef, 1e9), axis=1, keepdims=True)
    hit1 = lanef == i1
    rest = jnp.where(ise, jnp.where(hit1, -1.0, eprob), -1.0)
    p2 = jnp.max(rest, axis=1, keepdims=True)
    i2 = jnp.min(jnp.where(rest == p2, jnp.where(ise, jnp.where(hit1, 1e9, lanef), 1e9), 1e9),
                 axis=1, keepdims=True)
    hit2 = lanef == i2
    den = p1 + p2
    comb_ref[...] = jnp.where(hit1, g_w * (p1 / den), jnp.where(hit2, g_w * (p2 / den), 0.0))


def _outproj(x, yab, yc, w_out, g, b, wr, br, tm):
    n = x.shape[0]
    full = lambda i: (0, 0)
    wr_hi = wr.astype(BF16)
    wr_lo = (wr - wr_hi.astype(F32)).astype(BF16)
    return pl.pallas_call(
        _outproj_kernel,
        grid=(n // tm,),
        in_specs=[pl.BlockSpec((tm, D_MODEL), lambda i: (i, 0)),
                  pl.BlockSpec((tm, 512), lambda i: (i, 0)),
                  pl.BlockSpec((tm, 512), lambda i: (i, 0)),
                  pl.BlockSpec((D_MODEL, D_MODEL), full),
                  pl.BlockSpec((1, D_MODEL), full), pl.BlockSpec((1, D_MODEL), full),
                  pl.BlockSpec((D_MODEL, 128), full), pl.BlockSpec((D_MODEL, 128), full),
                  pl.BlockSpec((1, 128), full)],
        out_specs=[pl.BlockSpec((tm, D_MODEL), lambda i: (i, 0)),
                   pl.BlockSpec((tm, 128), lambda i: (i, 0))],
        out_shape=[jax.ShapeDtypeStruct((n, D_MODEL), F32), jax.ShapeDtypeStruct((n, 128), F32)],
        compiler_params=_cparams(("arbitrary",)),
        name="outproj_ln_route",
    )(x, yab, yc, w_out, g, b, wr_hi, wr_lo, br)


def _moe_kernel(x_ref, comb_ref, wgu_ref, wd_ref, g_ref, b_ref, o_ref, xb_ref, acc_ref):
    e = pl.program_id(1)

    @pl.when(e == 0)
    def _():
        xb_ref[...] = x_ref[...].astype(BF16)
        acc_ref[...] = jnp.zeros(acc_ref.shape, F32)

    gu = _dot(xb_ref[...], wgu_ref[0])
    hid = _silu(gu[:, 0:D_EXPERT]) * gu[:, D_EXPERT:2 * D_EXPERT]
    comb = comb_ref[...]
    lane = lax.broadcasted_iota(jnp.int32, comb.shape, 1)
    c = jnp.sum(jnp.where(lane == e, comb, 0.0), axis=1, keepdims=True)
    acc_ref[...] += _dot((hid * c).astype(BF16), wd_ref[0])

    @pl.when(e == pl.num_programs(1) - 1)
    def _():
        o_ref[...] = _ln(ALPHA * x_ref[...] + acc_ref[...], g_ref[...], b_ref[...])


def _moe(x1, comb, wgu, wd, g, b, tm):
    n = x1.shape[0]
    return pl.pallas_call(
        _moe_kernel,
        grid=(n // tm, N_EXPERTS),
        in_specs=[pl.BlockSpec((tm, D_MODEL), lambda i, e: (i, 0)),
                  pl.BlockSpec((tm, 128), lambda i, e: (i, 0)),
                  pl.BlockSpec((1, D_MODEL, 2 * D_EXPERT), lambda i, e: (e, 0, 0)),
                  pl.BlockSpec((1, D_EXPERT, D_MODEL), lambda i, e: (e, 0, 0)),
                  pl.BlockSpec((1, D_MODEL), lambda i, e: (0, 0)),
                  pl.BlockSpec((1, D_MODEL), lambda i, e: (0, 0))],
        out_specs=pl.BlockSpec((tm, D_MODEL), lambda i, e: (i, 0)),
        out_shape=jax.ShapeDtypeStruct((n, D_MODEL), F32),
        scratch_shapes=[pltpu.VMEM((tm, D_MODEL), BF16), pltpu.VMEM((tm, D_MODEL), F32)],
        compiler_params=_cparams(("arbitrary", "arbitrary")),
        name="moe_ffn_ln",
    )(x1, comb, wgu, wd, g, b)


def _sinproj_kernel(*refs, apply_ln):
    if apply_ln:
        (x_ref, lng_ref, lnb_ref, w_ref, caw_ref, cab_ref, calg_ref, calb_ref, cbw_ref, sta_ref, stb_ref,
         h_ref, q_ref, k_ref, v_ref, yab_ref, u_ref, ch_ref) = refs
    else:
        (x_ref, w_ref, caw_ref, cab_ref, calg_ref, calb_ref, cbw_ref, sta_ref, stb_ref,
         q_ref, k_ref, v_ref, yab_ref, u_ref, ch_ref) = refs
    x = x_ref[...]
    if apply_ln:
        x = _ln(x, lng_ref[...], lnb_ref[...])
        h_ref[...] = x
    xb = x.astype(BF16)

    def seg(a, b):
        return _dot(xb, w_ref[:, a:b])

    u = seg(0, 256) * jax.nn.sigmoid(seg(256, 512))
    u_ref[...] = u
    acc = caw_ref[CONF_WIDTH - 1:CONF_WIDTH, :] * u
    for j in range(CONF_WIDTH - 1):
        acc = acc + caw_ref[j:j + 1, :] * sta_ref[0, j]
    y_a = _silu(_ln(acc + cab_ref[...], calg_ref[...], calb_ref[...]))
    yab_ref[:, 0:256] = y_a.astype(BF16)

    ch = seg(768, 1024) * seg(1024, 1280)
    ch_ref[...] = ch
    accb = cbw_ref[SCONV_WIDTH - 1:SCONV_WIDTH, :] * ch
    for j in range(SCONV_WIDTH - 1):
        accb = accb + cbw_ref[j:j + 1, :] * stb_ref[0, j]
    yab_ref[:, 256:512] = (seg(512, 768) * accb).astype(BF16)

    q_ref[...] = seg(1280, 1792) * ATT_SCALE
    k_ref[...] = seg(1792, 2304)
    v_ref[...] = seg(2304, 2816)


def _sinproj(x, ln_g, ln_b, w_in, caw, cab, calg, calb, cbw, sta_t, stb_t, layer, apply_ln):
    n = x.shape[0]
    full = lambda i: (0, 0)
    in_specs = [pl.BlockSpec((n, D_MODEL), full)]
    args = [x]
    if apply_ln:
        in_specs += [pl.BlockSpec((1, D_MODEL), full), pl.BlockSpec((1, D_MODEL), full)]
        args += [ln_g, ln_b]
    in_specs += [pl.BlockSpec((D_MODEL, D_IN), full),
                 pl.BlockSpec((CONF_WIDTH, D_CONF), full),
                 pl.BlockSpec((1, D_CONF), full), pl.BlockSpec((1, D_CONF), full), pl.BlockSpec((1, D_CONF), full),
                 pl.BlockSpec((SCONV_WIDTH, D_SCONV), full),
                 pl.BlockSpec((1, CONF_WIDTH - 1, n, D_CONF), lambda i: (layer, 0, 0, 0)),
                 pl.BlockSpec((1, SCONV_WIDTH - 1, n, D_SCONV), lambda i: (layer, 0, 0, 0))]
    args += [w_in, caw, cab, calg, calb, cbw, sta_t, stb_t]
    out_specs, out_shape = [], []
    if apply_ln:
        out_specs.append(pl.BlockSpec((n, D_MODEL), full))
        out_shape.append(jax.ShapeDtypeStruct((n, D_MODEL), F32))
    out_specs += [pl.BlockSpec((n, D_ATT), full)] * 3 + [pl.BlockSpec((n, 512), full),
                                                        pl.BlockSpec((n, D_CONF), full),
                                                        pl.BlockSpec((n, D_SCONV), full)]
    out_shape += [jax.ShapeDtypeStruct((n, D_ATT), F32)] * 3 + [jax.ShapeDtypeStruct((n, 512), BF16),
                                                               jax.ShapeDtypeStruct((n, D_CONF), F32),
                                                               jax.ShapeDtypeStruct((n, D_SCONV), F32)]
    return pl.pallas_call(
        functools.partial(_sinproj_kernel, apply_ln=apply_ln),
        grid=(1,),
        in_specs=in_specs, out_specs=out_specs, out_shape=out_shape,
        compiler_params=_cparams(("arbitrary",)),
        name="sample_inproj_conv",
    )(*args)


N_PAGES = 16
N_PAST_BLK = N_PAGES * PAGE // BLK
SEL_STRIDE = 16


def _page_scores(kt_page, qcol):
    return jnp.sum(kt_page * qcol, axis=0, keepdims=True)


def _sgate_kernel(pt_ref, qt_ref, ptv_ref, *refs):
    krefs = refs[:N_PAGES]
    out_ref, sc_ref = refs[N_PAGES], refs[N_PAGES + 1]
    lane1 = lax.broadcasted_iota(jnp.int32, (1, LANES), 1)
    rows = []
    for hd in range(N_HEADS):
        qcol = jnp.broadcast_to(qt_ref[0, :, hd:hd + 1], (HEAD_DIM, PAGE))
        gh = jnp.zeros((1, LANES), F32)
        for n in range(N_PAST_BLK):
            sc0 = _page_scores(krefs[2 * n][0, 0, hd], qcol)
            sc1 = _page_scores(krefs[2 * n + 1][0, 0, hd], qcol)
            sc_ref[0, hd, 2 * n:2 * n + 1, :] = sc0
            sc_ref[0, hd, 2 * n + 1:2 * n + 2, :] = sc1
            gh = jnp.where(lane1 == n, jnp.sum(sc0 + sc1, axis=1, keepdims=True), gh)
        rows.append(gh)
    g = jnp.concatenate(rows, axis=0)
    lane = lax.broadcasted_iota(jnp.int32, (N_HEADS, LANES), 1)
    lanef = lane.astype(F32)
    g = jnp.where(lane < N_PAST_BLK, g, LOWEST)
    ptf = ptv_ref[0].astype(F32)
    l16 = lax.broadcasted_iota(jnp.int32, (N_HEADS, N_PAGES), 1).astype(F32)
    res = jnp.zeros((N_HEADS, LANES), F32)
    for r in range(TOPK):
        mx = jnp.max(g, axis=1, keepdims=True)
        idx = jnp.min(jnp.where(g == mx, lanef, 1e9), axis=1, keepdims=True)
        g = jnp.where(lanef == idx, LOWEST, g)
        pg0 = jnp.sum(jnp.where(l16 == 2.0 * idx, ptf, 0.0), axis=1, keepdims=True)
        pg1 = jnp.sum(jnp.where(l16 == 2.0 * idx + 1.0, ptf, 0.0), axis=1, keepdims=True)
        res = jnp.where(lane == 2 * r, pg0, res)
        res = jnp.where(lane == 2 * r + 1, pg1, res)
        res = jnp.where(lane == 8 + r, idx, res)
    out_ref[0] = res.astype(jnp.int32)


def _sgate(page_table, qt8, cache_kt, layer):
    nseq = page_table.shape[0]
    ptv = page_table.reshape(nseq, 1, N_PAGES)

    def kspec(j):
        return pl.BlockSpec((1, 1, N_HEADS, HEAD_DIM, PAGE), lambda b, pt: (layer, pt[b * N_PAGES + j], 0, 0, 0))

    grid_spec = pltpu.PrefetchScalarGridSpec(
        num_scalar_prefetch=1,
        grid=(nseq,),
        in_specs=[pl.BlockSpec((1, HEAD_DIM, N_HEADS), lambda b, pt: (b, 0, 0)),
                  pl.BlockSpec((1, 1, N_PAGES), lambda b, pt: (b, 0, 0))]
                 + [kspec(j) for j in range(N_PAGES)],
        out_specs=[pl.BlockSpec((1, N_HEADS, LANES), lambda b, pt: (b, 0, 0)),
                   pl.BlockSpec((1, N_HEADS, N_PAGES, PAGE), lambda b, pt: (b, 0, 0, 0))],
    )
    return pl.pallas_call(
        _sgate_kernel,
        grid_spec=grid_spec,
        out_shape=[jax.ShapeDtypeStruct((nseq, N_HEADS, LANES), jnp.int32),
                   jax.ShapeDtypeStruct((nseq, N_HEADS, N_PAGES, PAGE), F32)],
        compiler_params=_cparams(("arbitrary",)),
        name="sample_block_gate",
    )(page_table.reshape(-1), qt8, ptv, *([cache_kt] * N_PAGES))


N_SEL_PAGES = TOPK * BLK // PAGE


def _sattn_kernel(sel_ref, rb_ref, q_ref, kn_ref, vnt_ref, stab_ref, sc_ref, cv_ref, o_ref, vbuf_ref, sem_ref,
                  *, layer):
    b = pl.program_id(0)
    nseq = pl.num_programs(0)
    slot = lax.rem(b, 2)

    def page_copy(seq, to_slot, hd, s):
        page = sel_ref[(seq * N_HEADS + hd) * SEL_STRIDE + s]
        return pltpu.make_async_copy(cv_ref.at[layer, page, hd], vbuf_ref.at[to_slot, hd * N_SEL_PAGES + s],
                                     sem_ref.at[to_slot])

    def start_gather(seq, to_slot):
        for hd in range(N_HEADS):
            for s in range(N_SEL_PAGES):
                page_copy(seq, to_slot, hd, s).start()

    @pl.when(b == 0)
    def _():
        start_gather(0, 0)

    @pl.when(b + 1 < nseq)
    def _():
        start_gather(b + 1, 1 - slot)

    for hd in range(N_HEADS):
        for s in range(N_SEL_PAGES):
            page_copy(b, slot, hd, s).wait()

    s_self_all = jnp.sum(q_ref[0] * kn_ref[0], axis=1, keepdims=True)
    for hd in range(N_HEADS):
        base = (b * N_HEADS + hd) * SEL_STRIDE
        c31 = rb_ref[N_BUCKETS - 1, hd]
        scores = []
        for s in range(N_SEL_PAGES):
            blk = sel_ref[base + 8 + s // 2]
            sc = sc_ref[0, hd, pl.ds(2 * blk + s % 2, 1), :]
            tab = stab_ref[hd, (s % 2):(s % 2) + 1, :]
            scores.append(sc + jnp.where(blk == N_PAST_BLK - 1, tab, c31))
        s_self = s_self_all[hd:hd + 1, :] + rb_ref[0, hd]
        m = s_self
        for sc in scores:
            m = jnp.maximum(m, jnp.max(sc, axis=1, keepdims=True))
        e_self = jnp.exp(s_self - m)
        den = e_self
        acc = jnp.zeros((HEAD_DIM, PAGE), F32)
        for s in range(N_SEL_PAGES):
            e = jnp.exp(scores[s] - m)
            den = den + jnp.sum(e, axis=1, keepdims=True)
            acc = acc + vbuf_ref[slot, hd * N_SEL_PAGES + s] * e
        num = jnp.sum(acc, axis=1, keepdims=True) + e_self * vnt_ref[0, :, hd:hd + 1]
        o_ref[0, :, hd:hd + 1] = num / den


def _sattn(sel, rel_bias, q8, kn8, vnt8, stab, scores, cache_vt, layer):
    nseq = q8.shape[0]
    npg = N_HEADS * N_SEL_PAGES

    def row():
        return pl.BlockSpec((1, N_HEADS, HEAD_DIM), lambda b, sl: (b, 0, 0))

    def col():
        return pl.BlockSpec((1, HEAD_DIM, N_HEADS), lambda b, sl: (b, 0, 0))

    grid_spec = pltpu.PrefetchScalarGridSpec(
        num_scalar_prefetch=1,
        grid=(nseq,),
        in_specs=[pl.BlockSpec(memory_space=pltpu.SMEM), row(), row(), col(),
                  pl.BlockSpec((N_HEADS, 8, PAGE), lambda b, sl: (0, 0, 0)),
                  pl.BlockSpec((1, N_HEADS, N_PAGES, PAGE), lambda b, sl: (b, 0, 0, 0)),
                  pl.BlockSpec(memory_space=pl.ANY)],
        out_specs=col(),
        scratch_shapes=[pltpu.VMEM((2, npg, HEAD_DIM, PAGE), F32), pltpu.SemaphoreType.DMA((2,))],
    )
    return pl.pallas_call(
        functools.partial(_sattn_kernel, layer=layer),
        grid_spec=grid_spec,
        out_shape=jax.ShapeDtypeStruct((nseq, HEAD_DIM, N_HEADS), F32),
        compiler_params=_cparams(("arbitrary",)),
        name="sample_attention",
    )(sel.reshape(-1), rel_bias, q8, kn8, vnt8, stab, scores, cache_vt)


TM_OUT = 512
TM_MOE = 1024


def kernel(x_prompt, x_sample, cache_k, cache_v, state_conv_a, state_conv_b, page_table, ln_in_g, ln_in_b, w_in, conv_a_w, conv_a_b, conv_a_ln_g, conv_a_ln_b, conv_b_w, rel_bias, w_out, ln1_g, ln1_b, w_group, b_group, w_router, b_router, w_gate, w_up, w_down, ln2_g, ln2_b):
    bsz, t, _ = x_prompt.shape
    nseq = x_sample.shape[0]
    n_p = bsz * t
    row = lambda a: a.reshape(1, -1)

    btab, stab = _bias_tables(rel_bias)
    sta_t = state_conv_a.transpose(0, 2, 1, 3)
    stb_t = state_conv_b.transpose(0, 2, 1, 3)
    cache_kt = cache_k.transpose(0, 1, 2, 4, 3)
    cache_vt = cache_v.transpose(0, 1, 2, 4, 3)
    x_s = x_sample.reshape(nseq, D_MODEL)
    kpages = jnp.zeros((DEPTH, bsz, t // PAGE, N_HEADS, HEAD_DIM, PAGE), F32)
    vpages = jnp.zeros((DEPTH, bsz, t // PAGE, N_HEADS, HEAD_DIM, PAGE), F32)

    h_p = x_prompt
    h_s = x_s
    ap_l, bp_l, ks_l, vs_l, us_l, cs_l = [], [], [], [], [], []
    for l in range(DEPTH):
        first = l == 0
        w_in_b = w_in[l].astype(BF16)
        w_out_b = w_out[l].astype(BF16)
        wgu = jnp.concatenate([w_gate[l], w_up[l]], axis=-1).astype(BF16)
        wd = w_down[l].astype(BF16)
        wr = jnp.pad(jnp.concatenate([w_router[l], w_group[l]], axis=-1), ((0, 0), (0, 128 - N_EXPERTS - N_GROUPS)))
        br = jnp.pad(jnp.concatenate([b_router[l], b_group[l]]), (0, 128 - N_EXPERTS - N_GROUPS)).reshape(1, 128)
        conv_args = (conv_a_w[l], row(conv_a_b[l]), row(conv_a_ln_g[l]), row(conv_a_ln_b[l]), conv_b_w[l])

        outs = _inproj(h_p, row(ln_in_g), row(ln_in_b), w_in_b, *conv_args, kpages, vpages, l, apply_ln=first)
        if first:
            h_p, outs = outs[0], outs[1:]
        qa, kt, vb, kpages, vpages, yab, new_a, new_b = outs
        yc = _attention(qa, kt, vb, btab)
        x1, comb = _outproj(h_p.reshape(n_p, D_MODEL), yab.reshape(n_p, 512), yc.reshape(n_p, D_ATT),
                            w_out_b, row(ln1_g[l]), row(ln1_b[l]), wr, br, TM_OUT)
        h_p = _moe(x1, comb, wgu, wd, row(ln2_g[l]), row(ln2_b[l]), TM_MOE).reshape(bsz, t, D_MODEL)
        ap_l.append(new_a)
        bp_l.append(new_b)

        souts = _sinproj(h_s, row(ln_in_g), row(ln_in_b), w_in_b, *conv_args, sta_t, stb_t, l, apply_ln=first)
        if first:
            h_s, souts = souts[0], souts[1:]
        q_s, k_s, v_s, yab_s, u_s, ch_s = souts
        q8 = q_s.reshape(nseq, N_HEADS, HEAD_DIM)
        k8 = k_s.reshape(nseq, N_HEADS, HEAD_DIM)
        v8 = v_s.reshape(nseq, N_HEADS, HEAD_DIM)
        qt8 = q8.transpose(0, 2, 1)
        sel, scores_s = _sgate(page_table, qt8, cache_kt, l)
        yc_s = _sattn(sel[:, :, :SEL_STRIDE], rel_bias, q8, k8, v8.transpose(0, 2, 1), stab, scores_s, cache_vt, l)
        yc_s = yc_s.transpose(0, 2, 1).reshape(nseq, D_ATT).astype(BF16)
        x1_s, comb_s = _outproj(h_s, yab_s, yc_s, w_out_b, row(ln1_g[l]), row(ln1_b[l]), wr, br, nseq)
        h_s = _moe(x1_s, comb_s, wgu, wd, row(ln2_g[l]), row(ln2_b[l]), nseq)
        ks_l.append(k8.reshape(nseq, N_HEADS, 1, HEAD_DIM))
        vs_l.append(v8.reshape(nseq, N_HEADS, 1, HEAD_DIM))
        us_l.append(u_s)
        cs_l.append(ch_s)

    new_a_s = jnp.concatenate([state_conv_a[:, :, 1:], jnp.stack(us_l)[:, :, None, :]], axis=2)
    new_b_s = jnp.concatenate([state_conv_b[:, :, 1:], jnp.stack(cs_l)[:, :, None, :]], axis=2)
    return (h_p, h_s.reshape(nseq, 1, D_MODEL),
            kpages.transpose(0, 1, 2, 3, 5, 4), vpages.transpose(0, 1, 2, 3, 5, 4),
            jnp.stack(ap_l), jnp.stack(bp_l), jnp.stack(ks_l), jnp.stack(vs_l), new_a_s, new_b_s)
```
